```python
import math
import jax, jax.numpy as jnp
from jax import lax
import numpy as np

D_MODEL = 2048
BATCH = 4
SEQ = 4096
DEPTH = 1

CTX_LEN = 256
GRID_W = 64
EPS = 1e-6
N_MOD = 6
S5_WIDTH = 1024
S5_GROUP = 16
S5_GROUPS = S5_WIDTH // S5_GROUP
S5_STATE = 64
POOL_WIDTH = 1024
POOL_WINDOWS = (2, 4, 8, 16)
POOL_GROUPS = len(POOL_WINDOWS)
POOL_GROUP_W = POOL_WIDTH // POOL_GROUPS
N_BRANCH = 2
IN_WIDTH = S5_WIDTH + POOL_WIDTH + N_BRANCH * D_MODEL
PEER_HEADS = 8
PEER_KEYS = 128
PEER_EXPERTS = PEER_KEYS * PEER_KEYS
PEER_QDIM = 256
PEER_HALF = PEER_QDIM // 2
PEER_TOPK = 16
PEER_BLOCK = 128

kernel_name = "hybrid_s5_pool_peer_dit_block"


def rmsnorm(x, w):
    xf = x.astype(jnp.float32)
    y = xf * lax.rsqrt(jnp.mean(xf * xf, axis=-1, keepdims=True) + EPS)
    return (y * w.astype(jnp.float32)).astype(x.dtype)


def modulate(x, w, shift, scale):
    return rmsnorm(x, w) * (1 + scale) + shift


def s5_discretise(a_re, a_im, log_dt, b_re, b_im):
    a_re = a_re.astype(jnp.float32)
    a_im = a_im.astype(jnp.float32)
    b_re = b_re.astype(jnp.float32)
    b_im = b_im.astype(jnp.float32)
    dt = jnp.exp(log_dt.astype(jnp.float32))[:, None]
    mag = jnp.exp(dt * a_re)
    ab_re = mag * jnp.cos(dt * a_im)
    ab_im = mag * jnp.sin(dt * a_im)
    den = a_re * a_re + a_im * a_im
    nr = ab_re - 1.0
    ni = ab_im
    f_re = (nr * a_re + ni * a_im) / den
    f_im = (ni * a_re - nr * a_im) / den
    bb_re = f_re[..., None] * b_re - f_im[..., None] * b_im
    bb_im = f_re[..., None] * b_im + f_im[..., None] * b_re
    return ab_re, ab_im, bb_re, bb_im


def _complex_affine_combine(e1, e2):
    a1r, a1i, b1r, b1i = e1
    a2r, a2i, b2r, b2i = e2
    return (a2r * a1r - a2i * a1i,
            a2r * a1i + a2i * a1r,
            a2r * b1r - a2i * b1i + b2r,
            a2r * b1i + a2i * b1r + b2i)


def s5_scan(u, ab_re, ab_im, bb_re, bb_im, reverse, h0_re=None, h0_im=None):
    L = u.shape[1]
    b_re = jnp.einsum('blgh,gph->blgp', u, bb_re)
    b_im = jnp.einsum('blgh,gph->blgp', u, bb_im)
    a_re = jnp.broadcast_to(ab_re[None, None], (1, L) + ab_re.shape)
    a_im = jnp.broadcast_to(ab_im[None, None], (1, L) + ab_im.shape)
    p_re, p_im, h_re, h_im = lax.associative_scan(
        _complex_affine_combine, (a_re, a_im, b_re, b_im), reverse=reverse, axis=1)
    if h0_re is not None:
        h0_re = h0_re[:, None]
        h0_im = h0_im[:, None]
        h_re, h_im = (h_re + p_re * h0_re - p_im * h0_im,
                      h_im + p_re * h0_im + p_im * h0_re)
    return h_re, h_im


def s5_readout(h_re, h_im, c_re, c_im):
    return (jnp.einsum('blgp,ghp->blgh', h_re, c_re.astype(jnp.float32))
            - jnp.einsum('blgp,ghp->blgh', h_im, c_im.astype(jnp.float32)))


def s5_output(y, u, d_skip, w_glu, b_glu, dtype):
    Bn, L = u.shape[:2]
    y = y.reshape(Bn, L, S5_WIDTH) + d_skip.astype(jnp.float32) * u.reshape(Bn, L, S5_WIDTH)
    g = jax.nn.gelu(y)
    return (g * jax.nn.sigmoid(g @ w_glu.astype(jnp.float32) + b_glu.astype(jnp.float32))).astype(dtype)


def s5_branch(u_lat, u_ctx, a_re, a_im, log_dt, b_re, b_im, c_re, c_im, d_skip, w_glu, b_glu, need_ctx):
    Bn, L, _ = u_lat.shape
    Lc = u_ctx.shape[1]
    ul = u_lat.astype(jnp.float32).reshape(Bn, L, S5_GROUPS, S5_GROUP)
    uc = u_ctx.astype(jnp.float32).reshape(Bn, Lc, S5_GROUPS, S5_GROUP)
    y_lat = 0.0
    y_ctx = 0.0
    for d, rev in enumerate((False, True)):
        ab_re, ab_im, bb_re, bb_im = s5_discretise(a_re[d], a_im[d], log_dt[d], b_re[d], b_im[d])
        hc_re, hc_im = s5_scan(uc, ab_re, ab_im, bb_re, bb_im, rev)
        end = 0 if rev else Lc - 1
        hl_re, hl_im = s5_scan(ul, ab_re, ab_im, bb_re, bb_im, rev, hc_re[:, end], hc_im[:, end])
        y_lat = y_lat + s5_readout(hl_re, hl_im, c_re[d], c_im[d])
        if need_ctx:
            y_ctx = y_ctx + s5_readout(hc_re, hc_im, c_re[d], c_im[d])
    out_lat = s5_output(y_lat, ul, d_skip, w_glu, b_glu, u_lat.dtype)
    out_ctx = s5_output(y_ctx, uc, d_skip, w_glu, b_glu, u_ctx.dtype) if need_ctx else None
    return out_lat, out_ctx


def pool_branch(u, n_seg, seg_len, pool_w, pool_scale):
    Bn, L, _ = u.shape
    us = u.astype(jnp.float32).reshape(Bn, n_seg, seg_len, POOL_GROUPS, POOL_GROUP_W)
    cs = jnp.pad(jnp.cumsum(us, axis=2), ((0, 0), (0, 0), (1, 0), (0, 0), (0, 0)))
    pos = jnp.arange(seg_len)
    outs = []
    for g, w in enumerate(POOL_WINDOWS):
        lo = jnp.clip(pos - w // 2, 0, seg_len - 1)
        hi = jnp.clip(pos + w // 2 - 1, 0, seg_len - 1)
        cs_g = cs[:, :, :, g]
        win_sum = jnp.take(cs_g, hi + 1, axis=2) - jnp.take(cs_g, lo, axis=2)
        cnt = (hi - lo + 1).astype(jnp.float32)[:, None]
        outs.append(win_sum / cnt - us[:, :, :, g])
    p = jnp.stack(outs, axis=3)
    y = jnp.einsum('bnlgc,gcd->bnlgd', p, pool_w.astype(jnp.float32))
    return (y.reshape(Bn, L, POOL_WIDTH) * pool_scale.astype(jnp.float32)).astype(u.dtype)


def merge_branches(z, y_a, y_b, proj_a, proj_b, w_out):
    g0 = S5_WIDTH + POOL_WIDTH
    gate_a = jax.nn.sigmoid(z[..., g0:g0 + D_MODEL])
    gate_b = jax.nn.sigmoid(z[..., g0 + D_MODEL:g0 + 2 * D_MODEL])
    return (gate_a * (y_a @ proj_a) + gate_b * (y_b @ proj_b)) @ w_out


def peer_ffn(h, w_q, k1, k2, u_tab, v_tab):
    Bn, L, Dm = h.shape
    blocks = h.reshape((Bn * L) // PEER_BLOCK, PEER_BLOCK, Dm)

    def block_fn(hb):
        q = (hb @ w_q).reshape(PEER_BLOCK, PEER_HEADS, 2, PEER_HALF)
        s1 = jnp.einsum('thd,kd->thk', q[:, :, 0], k1).astype(jnp.float32)
        s2 = jnp.einsum('thd,kd->thk', q[:, :, 1], k2).astype(jnp.float32)
        v1, i1 = lax.top_k(s1, PEER_TOPK)
        v2, i2 = lax.top_k(s2, PEER_TOPK)
        cand = (v1[..., :, None] + v2[..., None, :]).reshape(PEER_BLOCK, PEER_HEADS, PEER_TOPK * PEER_TOPK)
        cid = (i1[..., :, None] * PEER_KEYS + i2[..., None, :]).reshape(PEER_BLOCK, PEER_HEADS, PEER_TOPK * PEER_TOPK)
        top_s, pos = lax.top_k(cand, PEER_TOPK)
        eid = jnp.take_along_axis(cid, pos, axis=-1)
        g = jax.nn.softmax(top_s, axis=-1)
        u_sel = jnp.take(u_tab, eid, axis=0)
        act = jax.nn.gelu(jnp.einsum('td,thkd->thk', hb, u_sel).astype(jnp.float32)) * g
        v_sel = jnp.take(v_tab, eid, axis=0)
        return jnp.einsum('thk,thkd->td', act.astype(hb.dtype), v_sel)

    return lax.map(block_fn, blocks).reshape(Bn, L, Dm)


def setup_inputs(seed: int = 0) -> dict:
    key = jax.random.key(seed)
    ks = jax.random.split(key, 32)

    def nrm(k, shape, s):
        return s * jax.random.normal(k, shape, jnp.float32)

    Ld, G, P, H = DEPTH, S5_GROUPS, S5_STATE, S5_GROUP
    n = jnp.arange(P, dtype=jnp.float32)
    return {
        "x": nrm(ks[0], (BATCH, SEQ, D_MODEL), 1.0),
        "c": nrm(ks[1], (BATCH, D_MODEL), 1.0),
        "ctx": nrm(ks[2], (BATCH, CTX_LEN, D_MODEL), 1.0),
        "c_ctx": nrm(ks[3], (D_MODEL,), 1.0),
        "w_mod": nrm(ks[4], (Ld, D_MODEL, N_MOD * D_MODEL), 0.5 * D_MODEL ** -0.5),
        "b_mod": nrm(ks[5], (Ld, N_MOD * D_MODEL), 0.02),
        "norm1_w": 1.0 + nrm(ks[6], (Ld, D_MODEL), 0.02),
        "norm2_w": 1.0 + nrm(ks[7], (Ld, D_MODEL), 0.02),
        "w_in": nrm(ks[8], (Ld, D_MODEL, IN_WIDTH), D_MODEL ** -0.5),
        "s5_a_re": -0.5 + nrm(ks[9], (Ld, 2, G, P), 0.01),
        "s5_a_im": math.pi * n + nrm(ks[10], (Ld, 2, G, P), 0.01),
        "s5_log_dt": jax.random.uniform(ks[11], (Ld, 2, G), jnp.float32, math.log(1e-3), math.log(1e-1)),
        "s5_b_re": nrm(ks[12], (Ld, 2, G, P, H), (2 * H) ** -0.5),
        "s5_b_im": nrm(ks[13], (Ld, 2, G, P, H), (2 * H) ** -0.5),
        "s5_c_re": nrm(ks[14], (Ld, 2, G, H, P), (2 * P) ** -0.5),
        "s5_c_im": nrm(ks[15], (Ld, 2, G, H, P), (2 * P) ** -0.5),
        "s5_d": nrm(ks[16], (Ld, S5_WIDTH), 0.5),
        "w_glu": nrm(ks[17], (Ld, S5_WIDTH, S5_WIDTH), S5_WIDTH ** -0.5),
        "b_glu": nrm(ks[18], (Ld, S5_WIDTH), 0.02),
        "pool_w": nrm(ks[19], (Ld, POOL_GROUPS, POOL_GROUP_W, POOL_GROUP_W), POOL_GROUP_W ** -0.5),
        "pool_scale": 1.0 + nrm(ks[20], (Ld, POOL_WIDTH), 0.05),
        "proj_a": nrm(ks[21], (Ld, S5_WIDTH, D_MODEL), S5_WIDTH ** -0.5),
        "proj_b": nrm(ks[22], (Ld, POOL_WIDTH, D_MODEL), POOL_WIDTH ** -0.5),
        "w_out": nrm(ks[23], (Ld, D_MODEL, D_MODEL), D_MODEL ** -0.5),
        "peer_wq": nrm(ks[24], (Ld, D_MODEL, PEER_HEADS * PEER_QDIM), D_MODEL ** -0.5),
        "peer_k1": nrm(ks[25], (Ld, PEER_KEYS, PEER_HALF), PEER_HALF ** -0.5),
        "peer_k2": nrm(ks[26], (Ld, PEER_KEYS, PEER_HALF), PEER_HALF ** -0.5),
        "peer_u": nrm(ks[27], (Ld, PEER_EXPERTS, D_MODEL), D_MODEL ** -0.5),
        "peer_v": nrm(ks[28], (Ld, PEER_EXPERTS, D_MODEL), 0.5),
        "final_w": 1.0 + nrm(ks[29], (D_MODEL,), 0.02),
    }


def reference(x, c, ctx, c_ctx, w_mod, b_mod, norm1_w, norm2_w, w_in,
              s5_a_re, s5_a_im, s5_log_dt, s5_b_re, s5_b_im, s5_c_re, s5_c_im, s5_d,
              w_glu, b_glu, pool_w, pool_scale, proj_a, proj_b, w_out,
              peer_wq, peer_k1, peer_k2, peer_u, peer_v, final_w):
    rows = x.shape[1] // GRID_W
    ctx_len = ctx.shape[1]
    for l in range(DEPTH):
        need_ctx = l < DEPTH - 1
        mod_lat = jax.nn.silu(c) @ w_mod[l] + b_mod[l]
        mod_ctx = jax.nn.silu(c_ctx) @ w_mod[l] + b_mod[l]
        sh1, sc1, g1, sh2, sc2, g2 = jnp.split(mod_lat[:, None, :], N_MOD, axis=-1)
        csh1, csc1, cg1, csh2, csc2, cg2 = jnp.split(mod_ctx, N_MOD, axis=-1)

        h = modulate(x, norm1_w[l], sh1, sc1)
        hc = modulate(ctx, norm1_w[l], csh1, csc1)
        z = h @ w_in[l]
        zc = hc @ w_in[l][:, :(IN_WIDTH if need_ctx else S5_WIDTH)]
        y_a, y_a_ctx = s5_branch(z[..., :S5_WIDTH], zc[..., :S5_WIDTH],
                                 s5_a_re[l], s5_a_im[l], s5_log_dt[l], s5_b_re[l], s5_b_im[l],
                                 s5_c_re[l], s5_c_im[l], s5_d[l], w_glu[l], b_glu[l], need_ctx)
        y_b = pool_branch(z[..., S5_WIDTH:S5_WIDTH + POOL_WIDTH], rows, GRID_W, pool_w[l], pool_scale[l])
        x = x + g1 * merge_branches(z, y_a, y_b, proj_a[l], proj_b[l], w_out[l])

        x = x + g2 * peer_ffn(modulate(x, norm2_w[l], sh2, sc2),
                              peer_wq[l], peer_k1[l], peer_k2[l], peer_u[l], peer_v[l])

        if need_ctx:
            y_b_ctx = pool_branch(zc[..., S5_WIDTH:S5_WIDTH + POOL_WIDTH], 1, ctx_len, pool_w[l], pool_scale[l])
            ctx = ctx + cg1 * merge_branches(zc, y_a_ctx, y_b_ctx, proj_a[l], proj_b[l], w_out[l])
            ctx = ctx + cg2 * peer_ffn(modulate(ctx, norm2_w[l], csh2, csc2),
                                       peer_wq[l], peer_k1[l], peer_k2[l], peer_u[l], peer_v[l])
    return rmsnorm(x, final_w)
```

```python
import functools
import math

import jax
import jax.numpy as jnp
from jax import lax
from jax.experimental import pallas as pl
from jax.experimental.pallas import tpu as pltpu

F32 = jnp.float32
BF16 = jnp.bfloat16
I32 = jnp.int32

LANES = 128
EPS = 1e-6
GRID_W = 64
S5_GROUP = 16
S5_STATE = 64
S5_GROUPS_PER_BLOCK = 8
POOL_WINDOWS = (2, 4, 8, 16)
PEER_HEADS = 8
PEER_KEYS = 128
PEER_TOPK = 16
N_MOD = 6

V7X_VMEM_BYTES = 64 * 1024 * 1024
VMEM_LIMIT = 56 * 1024 * 1024


def _params(sem, vmem=VMEM_LIMIT):
    return pltpu.CompilerParams(dimension_semantics=sem, vmem_limit_bytes=vmem)


def _gelu_tanh(x):
    return 0.5 * x * (1.0 + jnp.tanh(math.sqrt(2.0 / math.pi) * (x + 0.044715 * (x * x * x))))


def _sigmoid(x):
    return 1.0 / (1.0 + jnp.exp(-x))


def _mod_kernel(ct_ref, w_ref, b_ref, o_ref, *, n_rows):
    d_model, tn = w_ref.shape

    def body(kb, accs):
        k0 = pl.multiple_of(kb * 8, 8)
        wt = w_ref[pl.ds(k0, 8), :]
        a8 = ct_ref[pl.ds(k0, 8), :]
        a8 = a8 * _sigmoid(a8)
        return tuple(acc + a8[:, r:r + 1] * wt for r, acc in enumerate(accs))

    accs = lax.fori_loop(0, d_model // 8, body,
                         tuple(jnp.zeros((8, tn), F32) for _ in range(n_rows)))
    for r in range(8):
        if r < n_rows:
            o_ref[r:r + 1, :] = jnp.sum(accs[r], axis=0, keepdims=True) + b_ref[...]
        else:
            o_ref[r:r + 1, :] = jnp.zeros((1, tn), F32)


def _mod_call(c, c_ctx, w_mod, b_mod):
    bsz, d_model = c.shape
    n_out = w_mod.shape[1]
    n_rows = bsz + 1
    assert n_rows <= 8
    ct = jnp.concatenate([c, c_ctx[None], jnp.zeros((8 - n_rows, d_model), F32)], axis=0).T
    tn = 512
    return pl.pallas_call(
        functools.partial(_mod_kernel, n_rows=n_rows),
        grid=(n_out // tn,),
        in_specs=[pl.BlockSpec((d_model, 8), lambda j: (0, 0)),
                  pl.BlockSpec((d_model, tn), lambda j: (0, j)),
                  pl.BlockSpec((1, tn), lambda j: (0, j))],
        out_specs=pl.BlockSpec((8, tn), lambda j: (0, j)),
        out_shape=jax.ShapeDtypeStruct((8, n_out), F32),
        compiler_params=_params(("arbitrary",)),
        name="mod",
    )(ct, w_mod, b_mod[None])


def _modulated_norm(x, nw, shift, scale):
    ms = jnp.mean(x * x, axis=-1, keepdims=True)
    y = x * lax.rsqrt(ms + EPS) * nw
    return y * (1.0 + scale) + shift


def _inproj_kernel(x_ref, mod_ref, nw_ref, w_ref, *o_refs, d_model, widths, ctx_row):
    row = ctx_row if ctx_row is not None else pl.program_id(0)
    shift = mod_ref[pl.ds(row, 1), 0:d_model]
    scale = mod_ref[pl.ds(row, 1), d_model:2 * d_model]
    h = _modulated_norm(x_ref[0], nw_ref[...], shift, scale).astype(BF16)
    c0 = 0
    for o_ref, wd in zip(o_refs, widths):
        o_ref[0] = jnp.dot(h, w_ref[:, c0:c0 + wd], preferred_element_type=F32).astype(o_ref.dtype)
        c0 += wd


def _inproj_call(x, mod, norm_w, w_bf, widths, ctx_row, tm, name):
    bsz, seq, d_model = x.shape
    n_in = w_bf.shape[1]
    assert sum(widths) == n_in and seq % tm == 0
    kern = functools.partial(_inproj_kernel, d_model=d_model, widths=widths, ctx_row=ctx_row)
    return pl.pallas_call(
        kern,
        grid=(bsz, seq // tm),
        in_specs=[pl.BlockSpec((1, tm, d_model), lambda b, i: (b, i, 0)),
                  pl.BlockSpec(mod.shape, lambda b, i: (0, 0)),
                  pl.BlockSpec((1, d_model), lambda b, i: (0, 0)),
                  pl.BlockSpec((d_model, n_in), lambda b, i: (0, 0))],
        out_specs=[pl.BlockSpec((1, tm, wd), lambda b, i: (b, i, 0)) for wd in widths],
        out_shape=[jax.ShapeDtypeStruct((bsz, seq, wd), BF16) for wd in widths],
        compiler_params=_params(("arbitrary", "arbitrary")),
        name=name,
    )(x, mod, norm_w[None], w_bf)


def _disc_kernel(are_ref, aim_ref, ldt_ref, bre_ref, bim_ref, abr_ref, abi_ref, bbr_ref, bbi_ref):
    a_re = are_ref[...]
    a_im = aim_ref[...]
    dt = jnp.exp(ldt_ref[...])
    mag = jnp.exp(dt * a_re)
    ab_re = mag * jnp.cos(dt * a_im)
    ab_im = mag * jnp.sin(dt * a_im)
    den = a_re * a_re + a_im * a_im
    nr = ab_re - 1.0
    ni = ab_im
    f_re = (nr * a_re + ni * a_im) / den
    f_im = (ni * a_re - nr * a_im) / den
    b_re = bre_ref[...]
    b_im = bim_ref[...]
    abr_ref[...] = ab_re
    abi_ref[...] = ab_im
    bbr_ref[...] = f_re * b_re - f_im * b_im
    bbi_ref[...] = f_re * b_im + f_im * b_re


def _disc_call(a_re, a_im, log_dt, b_re, b_im):
    nd, g, p, h = b_re.shape
    rows, cols = nd * g, p * h
    rep = lambda a: jnp.broadcast_to(a[..., None], (nd, g, p, h)).reshape(rows, cols)
    args = (rep(a_re), rep(a_im),
            jnp.broadcast_to(log_dt[..., None, None], (nd, g, p, h)).reshape(rows, cols),
            b_re.reshape(rows, cols), b_im.reshape(rows, cols))
    outs = pl.pallas_call(
        _disc_kernel,
        out_shape=[jax.ShapeDtypeStruct((rows, cols), F32)] * 4,
        name="s5_disc",
    )(*args)
    ab_re, ab_im, bb_re, bb_im = (o.reshape(nd, g, p, h) for o in outs)
    return ab_re[..., 0], ab_im[..., 0], bb_re, bb_im


def _s5_weights(ab_re, ab_im, bb_re, bb_im, c_re, c_im):
    nd, g, p, h = bb_re.shape
    gb = S5_GROUPS_PER_BLOCK
    nb = g // gb
    eye = jnp.eye(gb, dtype=bool)

    def in_mat(bb):
        t = bb.reshape(nd, nb, gb, p, h).transpose(0, 1, 2, 4, 3)
        t = jnp.where(eye[None, None, :, None, :, None], t[:, :, :, :, None, :], 0.0)
        return t.reshape(nd, nb, gb * h, gb * p)

    def out_mat(cc):
        t = cc.reshape(nd, nb, gb, h, p).transpose(0, 1, 2, 4, 3)
        t = jnp.where(eye[None, None, :, None, :, None], t[:, :, :, :, None, :], 0.0)
        return t.reshape(nd, nb, gb * p, gb * h)

    w_b = jnp.concatenate([in_mat(bb_re), in_mat(bb_im)], axis=-1).astype(BF16)
    w_c = jnp.concatenate([out_mat(c_re.astype(F32)), -out_mat(c_im.astype(F32))], axis=-2).astype(BF16)
    a_r = ab_re.reshape(nd, nb, gb * p)
    a_i = ab_im.reshape(nd, nb, gb * p)
    return w_b, w_c, a_r, a_i


def _s5_kernel(zc_ref, zl_ref, wb_ref, wc_ref, ar_ref, ai_ref, y_ref, sbuf, h_ref,
               *, n_ctx_chunks, chunk, nb, half):
    d = pl.program_id(1)
    c = pl.program_id(2)
    cw = 2 * half
    bw = wb_ref.shape[2]

    @pl.when(c == 0)
    def _():
        h_ref[...] = jnp.zeros_like(h_ref)

    nl = cw // LANES
    hl = half // LANES

    def project(src_ref):
        for k in range(nb):
            xk = src_ref[0, :, k * bw:(k + 1) * bw]
            bu = jnp.dot(xk, wb_ref[0, k], preferred_element_type=F32)
            for l in range(nl):
                sbuf[l, pl.ds(k, chunk, stride=nb), :] = bu[:, l * LANES:(l + 1) * LANES]

    def scan():
        a_r = [ar_ref[0, :, l * LANES:(l + 1) * LANES] for l in range(hl)]
        a_i = [ai_ref[0, :, l * LANES:(l + 1) * LANES] for l in range(hl)]

        def step(s, carry):
            t = s + d * (chunk - 1 - 2 * s)
            row = pl.multiple_of(t * nb, nb)
            out = []
            for l in range(hl):
                h_r, h_i = carry[2 * l], carry[2 * l + 1]
                n_r = a_r[l] * h_r - a_i[l] * h_i + sbuf[l, pl.ds(row, nb), :]
                n_i = a_r[l] * h_i + a_i[l] * h_r + sbuf[hl + l, pl.ds(row, nb), :]
                sbuf[l, pl.ds(row, nb), :] = n_r
                sbuf[hl + l, pl.ds(row, nb), :] = n_i
                out += [n_r, n_i]
            return tuple(out)

        init = []
        for l in range(hl):
            init += [h_ref[l], h_ref[hl + l]]
        fin = lax.fori_loop(0, chunk, step, tuple(init), unroll=2)
        for l in range(hl):
            h_ref[l] = fin[2 * l]
            h_ref[hl + l] = fin[2 * l + 1]

    def readout():
        for k in range(nb):
            hk = jnp.concatenate([sbuf[l, pl.ds(k, chunk, stride=nb), :] for l in range(nl)], axis=1)
            y_ref[0, 0, :, k * bw:(k + 1) * bw] = jnp.dot(hk.astype(BF16), wc_ref[0, k],
                                                          preferred_element_type=F32)

    @pl.when(c < n_ctx_chunks)
    def _():
        project(zc_ref)
        scan()

    @pl.when(c >= n_ctx_chunks)
    def _():
        project(zl_ref)
        scan()
        readout()


def _s5_call(zc, zl, w_b, w_c, a_r, a_i, chunk):
    bsz, lc, width = zc.shape
    seq = zl.shape[1]
    nd, nb, bw, cw = w_b.shape
    half = cw // 2
    assert nb == 8 and lc % chunk == 0 and seq % chunk == 0 and width == nb * bw
    ncc, nlc = lc // chunk, seq // chunk

    def ctx_idx(b, d, c):
        cc = jnp.minimum(c, ncc - 1)
        return (b, cc + d * (ncc - 1 - 2 * cc), 0)

    def lat_chunk(d, c):
        lc_ = jnp.maximum(c - ncc, 0)
        return lc_ + d * (nlc - 1 - 2 * lc_)

    kern = functools.partial(_s5_kernel, n_ctx_chunks=ncc, chunk=chunk, nb=nb, half=half)
    return pl.pallas_call(
        kern,
        grid=(bsz, nd, ncc + nlc),
        in_specs=[pl.BlockSpec((1, chunk, width), ctx_idx),
                  pl.BlockSpec((1, chunk, width), lambda b, d, c: (b, lat_chunk(d, c), 0)),
                  pl.BlockSpec((1, nb, bw, cw), lambda b, d, c: (d, 0, 0, 0)),
                  pl.BlockSpec((1, nb, cw, bw), lambda b, d, c: (d, 0, 0, 0)),
                  pl.BlockSpec((1, nb, half), lambda b, d, c: (d, 0, 0)),
                  pl.BlockSpec((1, nb, half), lambda b, d, c: (d, 0, 0))],
        out_specs=pl.BlockSpec((1, 1, chunk, width), lambda b, d, c: (d, b, lat_chunk(d, c), 0)),
        out_shape=jax.ShapeDtypeStruct((nd, bsz, seq, width), F32),
        scratch_shapes=[pltpu.VMEM((cw // LANES, chunk * nb, LANES), F32),
                        pltpu.VMEM((cw // LANES, nb, LANES), F32)],
        compiler_params=_params(("arbitrary", "arbitrary", "arbitrary")),
        name="s5_scan",
    )(zc, zl, w_b, w_c, a_r, a_i)


def _mix_kernel(x_ref, yf_ref, yb_ref, us_ref, up_ref, gt_ref, mod_ref, dsk_ref, wglu_ref, bglu_ref,
                pw_ref, psc_ref, pa_ref, pb_ref, wo_ref, nw2_ref, wq_ref,
                x1_ref, h2_ref, q_ref, *, d_model, tm):
    b = pl.program_id(0)
    gate1 = mod_ref[pl.ds(b, 1), 2 * d_model:3 * d_model]
    shift2 = mod_ref[pl.ds(b, 1), 3 * d_model:4 * d_model]
    scale2 = mod_ref[pl.ds(b, 1), 4 * d_model:5 * d_model]

    u_s = us_ref[0].astype(F32)
    y = yf_ref[0, 0] + yb_ref[0, 0] + dsk_ref[...] * u_s
    ge = _gelu_tanh(y)
    lin = jnp.dot(ge.astype(BF16), wglu_ref[...], preferred_element_type=F32) + bglu_ref[...]
    y_a = (ge * _sigmoid(lin)).astype(BF16)

    u_p = up_ref[0]
    ri = lax.broadcasted_iota(I32, (tm, tm), 0)
    ci = lax.broadcasted_iota(I32, (tm, tm), 1)
    same_row = (ri // GRID_W) == (ci // GRID_W)
    pos_r = ri % GRID_W
    pos_c = ci % GRID_W
    pos1 = lax.broadcasted_iota(I32, (tm, 1), 0) % GRID_W
    gw = pw_ref.shape[1]
    y_b_parts = []
    for gi, w in enumerate(POOL_WINDOWS):
        lo = jnp.maximum(pos_r - w // 2, 0)
        hi = jnp.minimum(pos_r + w // 2 - 1, GRID_W - 1)
        band = jnp.where(same_row & (pos_c >= lo) & (pos_c <= hi), 1.0, 0.0).astype(BF16)
        ug = u_p[:, gi * gw:(gi + 1) * gw]
        win = jnp.dot(band, ug, preferred_element_type=F32)
        cnt = (jnp.minimum(pos1 + w // 2 - 1, GRID_W - 1) - jnp.maximum(pos1 - w // 2, 0) + 1).astype(F32)
        p = win / cnt - ug.astype(F32)
        y_b_parts.append(jnp.dot(p.astype(BF16), pw_ref[gi], preferred_element_type=F32))
    y_b = (jnp.concatenate(y_b_parts, axis=-1) * psc_ref[...]).astype(BF16)

    g_a = _sigmoid(gt_ref[0, :, 0:d_model].astype(F32))
    g_b = _sigmoid(gt_ref[0, :, d_model:2 * d_model].astype(F32))
    m = (g_a * jnp.dot(y_a, pa_ref[...], preferred_element_type=F32)
         + g_b * jnp.dot(y_b, pb_ref[...], preferred_element_type=F32))
    x1 = x_ref[0] + gate1 * jnp.dot(m.astype(BF16), wo_ref[...], preferred_element_type=F32)
    x1_ref[0] = x1

    h2 = _modulated_norm(x1, nw2_ref[...], shift2, scale2).astype(BF16)
    h2_ref[0] = h2
    q_ref[0] = jnp.dot(h2, wq_ref[...], preferred_element_type=F32)


def _mix_call(x, y2, us, up, gates, mod, d_skip, w_glu, b_glu, pool_w, pool_scale,
              proj_a, proj_b, w_out, norm2_w, w_q, tm):
    bsz, seq, d_model = x.shape
    sw = us.shape[2]
    pwid = up.shape[2]
    assert seq % tm == 0 and tm % GRID_W == 0
    full = lambda a: pl.BlockSpec(a.shape, lambda b, i: (0,) * a.ndim)
    row = lambda a: a[None]
    tok = lambda wd: pl.BlockSpec((1, tm, wd), lambda b, i: (b, i, 0))
    args = (x, y2, y2, us, up, gates, mod, row(d_skip), w_glu.astype(BF16), row(b_glu),
            pool_w.astype(BF16), row(pool_scale), proj_a.astype(BF16), proj_b.astype(BF16),
            w_out.astype(BF16), row(norm2_w), w_q.astype(BF16))
    in_specs = [tok(d_model),
                pl.BlockSpec((1, 1, tm, sw), lambda b, i: (0, b, i, 0)),
                pl.BlockSpec((1, 1, tm, sw), lambda b, i: (1, b, i, 0)),
                tok(sw), tok(pwid), tok(2 * d_model)] + [full(a) for a in args[6:]]
    return pl.pallas_call(
        functools.partial(_mix_kernel, d_model=d_model, tm=tm),
        grid=(bsz, seq // tm),
        in_specs=in_specs,
        out_specs=[tok(d_model), tok(d_model), tok(d_model)],
        out_shape=[jax.ShapeDtypeStruct((bsz, seq, d_model), F32),
                   jax.ShapeDtypeStruct((bsz, seq, d_model), BF16),
                   jax.ShapeDtypeStruct((bsz, seq, d_model), F32)],
        compiler_params=_params(("arbitrary", "arbitrary")),
        name="mix",
    )(*args)


def _candidate_plan():
    plan = []
    for p in range(8):
        n = PEER_TOPK // (p + 1)
        for q0 in range(0, n, 8):
            plan.append((p, q0, min(8, n - q0)))
    return plan


def _route_kernel(q_ref, k1_ref, k2_ref, i1_ref, i2_ref, g_ref,
                  v1s, x1s, v2s, x2s, e1s, e2s, gws, *, tq):
    neg = -jnp.inf
    nk = PEER_KEYS
    kio = lax.broadcasted_iota(I32, (nk, tq), 0)
    sub8 = lax.broadcasted_iota(I32, (8, tq), 0)
    plan = _candidate_plan()
    n_cand = 8 * (len(plan) + 1)
    cio = lax.broadcasted_iota(I32, (n_cand, tq), 0)
    hd = q_ref.shape[1] // PEER_HEADS
    half = hd // 2

    def top16(s, vals_ref, idx_ref):
        for k in range(PEER_TOPK):
            m = jnp.max(s, axis=0, keepdims=True)
            i = jnp.min(jnp.where(s == m, kio, nk), axis=0, keepdims=True)
            vals_ref[k:k + 1, :] = m
            idx_ref[k:k + 1, :] = i
            s = jnp.where(kio == i, neg, s)

    for h in range(PEER_HEADS):
        q1 = q_ref[:, h * hd:h * hd + half]
        q2 = q_ref[:, h * hd + half:(h + 1) * hd]
        nt = (((1,), (1,)), ((), ()))
        s1 = lax.dot_general(k1_ref[...], q1, nt, precision=lax.Precision.HIGHEST,
                             preferred_element_type=F32)
        s2 = lax.dot_general(k2_ref[...], q2, nt, precision=lax.Precision.HIGHEST,
                             preferred_element_type=F32)
        top16(s1, v1s, x1s)
        top16(s2, v2s, x2s)

        cand, cid1, cid2 = [], [], []
        for (p, q0, n) in plan:
            piece = v1s[p:p + 1, :] + v2s[q0:q0 + 8, :]
            if n < 8:
                piece = jnp.where(sub8 < n, piece, neg)
            cand.append(piece)
            cid1.append(jnp.broadcast_to(x1s[p:p + 1, :], (8, tq)))
            cid2.append(x2s[q0:q0 + 8, :])
        cand.append(v1s[8:16, :] + v2s[0:1, :])
        cid1.append(x1s[8:16, :])
        cid2.append(jnp.broadcast_to(x2s[0:1, :], (8, tq)))
        cand = jnp.concatenate(cand, axis=0)
        cid1 = jnp.concatenate(cid1, axis=0)
        cid2 = jnp.concatenate(cid2, axis=0)

        tops = []
        for k in range(PEER_TOPK):
            m = jnp.max(cand, axis=0, keepdims=True)
            i = jnp.min(jnp.where(cand == m, cio, n_cand), axis=0, keepdims=True)
            hit = cio == i
            r = h * PEER_TOPK + k
            e1s[r:r + 1, :] = jnp.sum(jnp.where(hit, cid1, 0), axis=0, keepdims=True)
            e2s[r:r + 1, :] = jnp.sum(jnp.where(hit, cid2, 0), axis=0, keepdims=True)
            tops.append(m)
            cand = jnp.where(hit, neg, cand)
        ex = [jnp.exp(t - tops[0]) for t in tops]
        tot = ex[0]
        for e in ex[1:]:
            tot = tot + e
        for k in range(PEER_TOPK):
            r = h * PEER_TOPK + k
            gws[r:r + 1, :] = ex[k] / tot

    i1_ref[...] = e1s[...].T
    i2_ref[...] = e2s[...].T
    g_ref[...] = gws[...].T


def _route_call(q2d, k1, k2, tq):
    n_tok, d_q = q2d.shape
    ns = PEER_HEADS * PEER_TOPK
    assert n_tok % tq == 0 and ns == 128 and tq == 128
    tok = pl.BlockSpec((tq, ns), lambda i: (i, 0))
    return pl.pallas_call(
        functools.partial(_route_kernel, tq=tq),
        grid=(n_tok // tq,),
        in_specs=[pl.BlockSpec((tq, d_q), lambda i: (i, 0)),
                  pl.BlockSpec(k1.shape, lambda i: (0, 0)),
                  pl.BlockSpec(k2.shape, lambda i: (0, 0))],
        out_specs=[tok, tok, tok],
        out_shape=[jax.ShapeDtypeStruct((n_tok, ns), I32),
                   jax.ShapeDtypeStruct((n_tok, ns), I32),
                   jax.ShapeDtypeStruct((n_tok, ns), F32)],
        scratch_shapes=[pltpu.VMEM((PEER_TOPK, tq), F32), pltpu.VMEM((PEER_TOPK, tq), I32),
                        pltpu.VMEM((PEER_TOPK, tq), F32), pltpu.VMEM((PEER_TOPK, tq), I32),
                        pltpu.VMEM((ns, tq), I32), pltpu.VMEM((ns, tq), I32),
                        pltpu.VMEM((ns, tq), F32)],
        compiler_params=_params(("arbitrary",)),
        name="peer_route",
    )(q2d, k1, k2)


def _gbuild_kernel(i1_ref, i2_ref, g_ref, o_ref, *, tg):
    nk = PEER_KEYS
    sub = lax.broadcasted_iota(I32, (nk, i1_ref.shape[1]), 0)

    def body(t, carry):
        i1 = i1_ref[pl.ds(t, 1), :]
        i2 = i2_ref[pl.ds(t, 1), :]
        gw = g_ref[pl.ds(t, 1), :]
        g_hi = gw.astype(BF16).astype(F32)
        g_lo = gw - g_hi
        hit1 = sub == i1
        a2 = jnp.concatenate([jnp.where(hit1, g_hi, 0.0), jnp.where(hit1, g_lo, 0.0)], axis=1).astype(BF16)
        b1 = jnp.where(sub == i2, 1.0, 0.0)
        b2 = jnp.concatenate([b1, b1], axis=1).astype(BF16)
        o_ref[t] = lax.dot_general(a2, b2, (((1,), (1,)), ((), ())), preferred_element_type=F32)
        return carry

    lax.fori_loop(0, tg, body, 0)


def _gbuild_call(i1, i2, g, tg):
    n_tok, ns = i1.shape
    nk = PEER_KEYS
    assert n_tok % tg == 0
    tok = pl.BlockSpec((tg, ns), lambda i: (i, 0))
    return pl.pallas_call(
        functools.partial(_gbuild_kernel, tg=tg),
        grid=(n_tok // tg,),
        in_specs=[tok, tok, tok],
        out_specs=pl.BlockSpec((tg, nk, nk), lambda i: (i, 0, 0)),
        out_shape=jax.ShapeDtypeStruct((n_tok, nk, nk), F32),
        compiler_params=_params(("arbitrary",)),
        name="peer_gbuild",
    )(i1, i2, g)


def _peer_kernel(h_ref, u_ref, v_ref, g_ref, x1_ref, mod_ref, fw_ref, o_ref, acc_ref,
                 *, d_model, rows_per_block, tokens_per_batch, tm):
    i = pl.program_id(0)
    j = pl.program_id(1)

    @pl.when(j == 0)
    def _():
        acc_ref[...] = jnp.zeros_like(acc_ref)

    z = lax.dot_general(h_ref[...], u_ref[...], (((1,), (1,)), ((), ())),
                        preferred_element_type=F32)
    gmat = jnp.concatenate([g_ref[:, r, :] for r in range(rows_per_block)], axis=1)
    w = (_gelu_tanh(z) * gmat).astype(BF16)
    acc_ref[...] += jnp.dot(w, v_ref[...], preferred_element_type=F32)

    @pl.when(j == pl.num_programs(1) - 1)
    def _():
        b = (i * tm) // tokens_per_batch
        gate2 = mod_ref[pl.ds(b, 1), 5 * d_model:6 * d_model]
        x2 = x1_ref[...] + gate2 * acc_ref[...]
        ms = jnp.mean(x2 * x2, axis=-1, keepdims=True)
        o_ref[...] = x2 * lax.rsqrt(ms + EPS) * fw_ref[...]


def _peer_call(h2, u_bf, v_bf, gmat, x1, mod, final_w, tokens_per_batch, tm, rows_per_block):
    n_tok, d_model = h2.shape
    n_exp = u_bf.shape[0]
    nk = PEER_KEYS
    eb = rows_per_block * nk
    assert n_tok % tm == 0 and n_exp % eb == 0 and tokens_per_batch % tm == 0
    kern = functools.partial(_peer_kernel, d_model=d_model, rows_per_block=rows_per_block,
                             tokens_per_batch=tokens_per_batch, tm=tm)
    return pl.pallas_call(
        kern,
        grid=(n_tok // tm, n_exp // eb),
        in_specs=[pl.BlockSpec((tm, d_model), lambda i, j: (i, 0)),
                  pl.BlockSpec((eb, d_model), lambda i, j: (j, 0)),
                  pl.BlockSpec((eb, d_model), lambda i, j: (j, 0)),
                  pl.BlockSpec((tm, rows_per_block, nk), lambda i, j: (i, j, 0)),
                  pl.BlockSpec((tm, d_model), lambda i, j: (i, 0)),
                  pl.BlockSpec(mod.shape, lambda i, j: (0, 0)),
                  pl.BlockSpec((1, d_model), lambda i, j: (0, 0))],
        out_specs=pl.BlockSpec((tm, d_model), lambda i, j: (i, 0)),
        out_shape=jax.ShapeDtypeStruct((n_tok, d_model), F32),
        scratch_shapes=[pltpu.VMEM((tm, d_model), F32)],
        compiler_params=_params(("arbitrary", "arbitrary")),
        name="peer_experts",
    )(h2, u_bf, v_bf, gmat, x1, mod, final_w[None])


def kernel(x, c, ctx, c_ctx, w_mod, b_mod, norm1_w, norm2_w, w_in, s5_a_re, s5_a_im, s5_log_dt,
           s5_b_re, s5_b_im, s5_c_re, s5_c_im, s5_d, w_glu, b_glu, pool_w, pool_scale, proj_a,
           proj_b, w_out, peer_wq, peer_k1, peer_k2, peer_u, peer_v, final_w):
    bsz, seq, d_model = x.shape
    depth = w_mod.shape[0]
    assert depth == 1, "only the single-layer configuration is implemented"
    l = 0
    s5_width = s5_d.shape[1]
    pool_width = pool_scale.shape[1]

    mod = _mod_call(c, c_ctx, w_mod[l], b_mod[l])

    w_in_bf = w_in[l].astype(BF16)
    tm = min(256, seq)
    us, up, gates = _inproj_call(x, mod, norm1_w[l], w_in_bf,
                                 (s5_width, pool_width, 2 * d_model), None, tm, "inproj_lat")
    lc = ctx.shape[1]
    (zc,) = _inproj_call(ctx, mod, norm1_w[l], w_in_bf[:, :s5_width], (s5_width,), bsz,
                         min(256, lc), "inproj_ctx")

    ab_re, ab_im, bb_re, bb_im = _disc_call(s5_a_re[l], s5_a_im[l], s5_log_dt[l], s5_b_re[l], s5_b_im[l])
    w_b, w_c, a_r, a_i = _s5_weights(ab_re, ab_im, bb_re, bb_im, s5_c_re[l], s5_c_im[l])
    chunk = min(256, lc, seq)
    y2 = _s5_call(zc, us, w_b, w_c, a_r, a_i, chunk)

    x1, h2, q = _mix_call(x, y2, us, up, gates, mod, s5_d[l], w_glu[l], b_glu[l], pool_w[l],
                          pool_scale[l], proj_a[l], proj_b[l], w_out[l], norm2_w[l], peer_wq[l], tm)

    n_tok = bsz * seq
    i1, i2, gw = _route_call(q.reshape(n_tok, d_model), peer_k1[l], peer_k2[l], 128)
    gmat = _gbuild_call(i1, i2, gw, 64)
    out = _peer_call(h2.reshape(n_tok, d_model), peer_u[l].astype(BF16), peer_v[l].astype(BF16),
                     gmat, x1.reshape(n_tok, d_model), mod, final_w, seq, min(512, seq), 8)
    return out.reshape(bsz, seq, d_model)
```

```python
import functools
import math

import jax
import jax.numpy as jnp
from jax import lax
from jax.experimental import pallas as pl
from jax.experimental.pallas import tpu as pltpu

F32 = jnp.float32
BF16 = jnp.bfloat16
I32 = jnp.int32

LANES = 128
EPS = 1e-6
GRID_W = 64
S5_GROUP = 16
S5_STATE = 64
S5_GROUPS_PER_BLOCK = 8
POOL_WINDOWS = (2, 4, 8, 16)
PEER_HEADS = 8
PEER_KEYS = 128
PEER_TOPK = 16
N_MOD = 6

V7X_VMEM_BYTES = 64 * 1024 * 1024
VMEM_LIMIT = 56 * 1024 * 1024


def _params(sem, vmem=VMEM_LIMIT):
    return pltpu.CompilerParams(dimension_semantics=sem, vmem_limit_bytes=vmem)


def _gelu_tanh(x):
    return 0.5 * x * (1.0 + jnp.tanh(math.sqrt(2.0 / math.pi) * (x + 0.044715 * (x * x * x))))


def _sigmoid(x):
    return 1.0 / (1.0 + jnp.exp(-x))


def _mod_kernel(cb_ref, w_ref, b_ref, o_ref, a_scr, *, n_rows):
    d_model, tn = w_ref.shape
    reps = tn // LANES

    @pl.when(pl.program_id(0) == 0)
    def _():
        cb = cb_ref[...]
        a_scr[...] = cb * _sigmoid(cb)

    def body(kb, accs):
        k0 = pl.multiple_of(kb * 8, 8)
        wt = w_ref[pl.ds(k0, 8), :]
        return tuple(acc + jnp.concatenate([a_scr[r, pl.ds(k0, 8), :]] * reps, axis=1) * wt
                     for r, acc in enumerate(accs))

    accs = lax.fori_loop(0, d_model // 8, body,
                         tuple(jnp.zeros((8, tn), F32) for _ in range(n_rows)), unroll=2)
    for r in range(8):
        if r < n_rows:
            o_ref[r:r + 1, :] = jnp.sum(accs[r], axis=0, keepdims=True) + b_ref[...]
        else:
            o_ref[r:r + 1, :] = jnp.zeros((1, tn), F32)


def _mod_call(c, c_ctx, w_mod, b_mod):
    bsz, d_model = c.shape
    n_out = w_mod.shape[1]
    n_rows = bsz + 1
    assert n_rows <= 8
    cb = jnp.broadcast_to(jnp.concatenate([c, c_ctx[None]], axis=0)[:, :, None], (n_rows, d_model, LANES))
    tn = 512
    return pl.pallas_call(
        functools.partial(_mod_kernel, n_rows=n_rows),
        grid=(n_out // tn,),
        in_specs=[pl.BlockSpec((n_rows, d_model, LANES), lambda j: (0, 0, 0)),
                  pl.BlockSpec((d_model, tn), lambda j: (0, j)),
                  pl.BlockSpec((1, tn), lambda j: (0, j))],
        out_specs=pl.BlockSpec((8, tn), lambda j: (0, j)),
        out_shape=jax.ShapeDtypeStruct((8, n_out), F32),
        scratch_shapes=[pltpu.VMEM((n_rows, d_model, LANES), F32)],
        compiler_params=_params(("arbitrary",)),
        name="mod",
    )(cb, w_mod, b_mod[None])


def _modulated_norm(x, nw, shift, scale):
    ms = jnp.mean(x * x, axis=-1, keepdims=True)
    y = x * lax.rsqrt(ms + EPS) * nw
    return y * (1.0 + scale) + shift


def _inproj_kernel(x_ref, mod_ref, nw_ref, w_ref, *o_refs, d_model, widths, ctx_row):
    row = ctx_row if ctx_row is not None else pl.program_id(0)
    shift = mod_ref[pl.ds(row, 1), 0:d_model]
    scale = mod_ref[pl.ds(row, 1), d_model:2 * d_model]
    h = _modulated_norm(x_ref[0], nw_ref[...], shift, scale).astype(BF16)
    c0 = 0
    for o_ref, wd in zip(o_refs, widths):
        o_ref[0] = jnp.dot(h, w_ref[:, c0:c0 + wd], preferred_element_type=F32).astype(o_ref.dtype)
        c0 += wd


def _inproj_call(x, mod, norm_w, w_bf, widths, ctx_row, tm, name):
    bsz, seq, d_model = x.shape
    n_in = w_bf.shape[1]
    assert sum(widths) == n_in and seq % tm == 0
    kern = functools.partial(_inproj_kernel, d_model=d_model, widths=widths, ctx_row=ctx_row)
    return pl.pallas_call(
        kern,
        grid=(bsz, seq // tm),
        in_specs=[pl.BlockSpec((1, tm, d_model), lambda b, i: (b, i, 0)),
                  pl.BlockSpec(mod.shape, lambda b, i: (0, 0)),
                  pl.BlockSpec((1, d_model), lambda b, i: (0, 0)),
                  pl.BlockSpec((d_model, n_in), lambda b, i: (0, 0))],
        out_specs=[pl.BlockSpec((1, tm, wd), lambda b, i: (b, i, 0)) for wd in widths],
        out_shape=[jax.ShapeDtypeStruct((bsz, seq, wd), BF16) for wd in widths],
        compiler_params=_params(("arbitrary", "arbitrary")),
        name=name,
    )(x, mod, norm_w[None], w_bf)


def _disc_kernel(are_ref, aim_ref, ldt_ref, bre_ref, bim_ref, abr_ref, abi_ref, bbr_ref, bbi_ref):
    a_re = are_ref[...]
    a_im = aim_ref[...]
    dt = jnp.exp(ldt_ref[...])
    mag = jnp.exp(dt * a_re)
    ab_re = mag * jnp.cos(dt * a_im)
    ab_im = mag * jnp.sin(dt * a_im)
    den = a_re * a_re + a_im * a_im
    nr = ab_re - 1.0
    ni = ab_im
    f_re = (nr * a_re + ni * a_im) / den
    f_im = (ni * a_re - nr * a_im) / den
    b_re = bre_ref[...]
    b_im = bim_ref[...]
    abr_ref[...] = ab_re
    abi_ref[...] = ab_im
    bbr_ref[...] = f_re * b_re - f_im * b_im
    bbi_ref[...] = f_re * b_im + f_im * b_re


def _disc_call(a_re, a_im, log_dt, b_re, b_im):
    nd, g, p, h = b_re.shape
    rows, cols = nd * g, p * h
    rep = lambda a: jnp.broadcast_to(a[..., None], (nd, g, p, h)).reshape(rows, cols)
    args = (rep(a_re), rep(a_im),
            jnp.broadcast_to(log_dt[..., None, None], (nd, g, p, h)).reshape(rows, cols),
            b_re.reshape(rows, cols), b_im.reshape(rows, cols))
    outs = pl.pallas_call(
        _disc_kernel,
        out_shape=[jax.ShapeDtypeStruct((rows, cols), F32)] * 4,
        name="s5_disc",
    )(*args)
    ab_re, ab_im, bb_re, bb_im = (o.reshape(nd, g, p, h) for o in outs)
    return ab_re[..., 0], ab_im[..., 0], bb_re, bb_im


def _s5_weights(ab_re, ab_im, bb_re, bb_im, c_re, c_im):
    nd, g, p, h = bb_re.shape
    gb = S5_GROUPS_PER_BLOCK
    nb = g // gb
    eye = jnp.eye(gb, dtype=bool)

    def in_mat(bb):
        t = bb.reshape(nd, nb, gb, p, h).transpose(0, 1, 2, 4, 3)
        t = jnp.where(eye[None, None, :, None, :, None], t[:, :, :, :, None, :], 0.0)
        return t.reshape(nd, nb, gb * h, gb * p)

    def out_mat(cc):
        t = cc.reshape(nd, nb, gb, h, p).transpose(0, 1, 2, 4, 3)
        t = jnp.where(eye[None, None, :, None, :, None], t[:, :, :, :, None, :], 0.0)
        return t.reshape(nd, nb, gb * p, gb * h)

    w_b = jnp.concatenate([in_mat(bb_re), in_mat(bb_im)], axis=-1).astype(BF16)
    w_c = jnp.concatenate([out_mat(c_re.astype(F32)), -out_mat(c_im.astype(F32))], axis=-2).astype(BF16)
    a_r = ab_re.reshape(nd, nb, gb * p)
    a_i = ab_im.reshape(nd, nb, gb * p)
    return w_b, w_c, a_r, a_i


def _s5_kernel(zc_ref, zl_ref, wb_ref, wc_ref, ar_ref, ai_ref, y_ref, sbuf, h_ref,
               *, n_ctx_chunks, chunk, nb, half):
    d = pl.program_id(1)
    c = pl.program_id(2)
    cw = 2 * half
    bw = wb_ref.shape[2]

    @pl.when(c == 0)
    def _():
        h_ref[...] = jnp.zeros_like(h_ref)

    nl = cw // LANES
    hl = half // LANES

    def project(src_ref):
        for k in range(nb):
            xk = src_ref[0, :, k * bw:(k + 1) * bw]
            bu = jnp.dot(xk, wb_ref[0, k], preferred_element_type=F32)
            for l in range(nl):
                sbuf[l, pl.ds(k, chunk, stride=nb), :] = bu[:, l * LANES:(l + 1) * LANES]

    def scan():
        a_r = [ar_ref[0, :, l * LANES:(l + 1) * LANES] for l in range(hl)]
        a_i = [ai_ref[0, :, l * LANES:(l + 1) * LANES] for l in range(hl)]

        def step(s, carry):
            t = s + d * (chunk - 1 - 2 * s)
            row = pl.multiple_of(t * nb, nb)
            out = []
            for l in range(hl):
                h_r, h_i = carry[2 * l], carry[2 * l + 1]
                n_r = a_r[l] * h_r - a_i[l] * h_i + sbuf[l, pl.ds(row, nb), :]
                n_i = a_r[l] * h_i + a_i[l] * h_r + sbuf[hl + l, pl.ds(row, nb), :]
                sbuf[l, pl.ds(row, nb), :] = n_r
                sbuf[hl + l, pl.ds(row, nb), :] = n_i
                out += [n_r, n_i]
            return tuple(out)

        init = []
        for l in range(hl):
            init += [h_ref[l], h_ref[hl + l]]
        fin = lax.fori_loop(0, chunk, step, tuple(init), unroll=2)
        for l in range(hl):
            h_ref[l] = fin[2 * l]
            h_ref[hl + l] = fin[2 * l + 1]

    def readout():
        for k in range(nb):
            hk = jnp.concatenate([sbuf[l, pl.ds(k, chunk, stride=nb), :] for l in range(nl)], axis=1)
            y_ref[0, 0, :, k * bw:(k + 1) * bw] = jnp.dot(hk.astype(BF16), wc_ref[0, k],
                                                          preferred_element_type=F32)

    @pl.when(c < n_ctx_chunks)
    def _():
        project(zc_ref)
        scan()

    @pl.when(c >= n_ctx_chunks)
    def _():
        project(zl_ref)
        scan()
        readout()


def _s5_call(zc, zl, w_b, w_c, a_r, a_i, chunk):
    bsz, lc, width = zc.shape
    seq = zl.shape[1]
    nd, nb, bw, cw = w_b.shape
    half = cw // 2
    assert nb == 8 and lc % chunk == 0 and seq % chunk == 0 and width == nb * bw
    ncc, nlc = lc // chunk, seq // chunk

    def ctx_idx(b, d, c):
        cc = jnp.minimum(c, ncc - 1)
        return (b, cc + d * (ncc - 1 - 2 * cc), 0)

    def lat_chunk(d, c):
        lc_ = jnp.maximum(c - ncc, 0)
        return lc_ + d * (nlc - 1 - 2 * lc_)

    kern = functools.partial(_s5_kernel, n_ctx_chunks=ncc, chunk=chunk, nb=nb, half=half)
    return pl.pallas_call(
        kern,
        grid=(bsz, nd, ncc + nlc),
        in_specs=[pl.BlockSpec((1, chunk, width), ctx_idx),
                  pl.BlockSpec((1, chunk, width), lambda b, d, c: (b, lat_chunk(d, c), 0)),
                  pl.BlockSpec((1, nb, bw, cw), lambda b, d, c: (d, 0, 0, 0)),
                  pl.BlockSpec((1, nb, cw, bw), lambda b, d, c: (d, 0, 0, 0)),
                  pl.BlockSpec((1, nb, half), lambda b, d, c: (d, 0, 0)),
                  pl.BlockSpec((1, nb, half), lambda b, d, c: (d, 0, 0))],
        out_specs=pl.BlockSpec((1, 1, chunk, width), lambda b, d, c: (d, b, lat_chunk(d, c), 0)),
        out_shape=jax.ShapeDtypeStruct((nd, bsz, seq, width), F32),
        scratch_shapes=[pltpu.VMEM((cw // LANES, chunk * nb, LANES), F32),
                        pltpu.VMEM((cw // LANES, nb, LANES), F32)],
        compiler_params=_params(("arbitrary", "arbitrary", "arbitrary")),
        name="s5_scan",
    )(zc, zl, w_b, w_c, a_r, a_i)


def _mix_kernel(x_ref, yf_ref, yb_ref, us_ref, up_ref, gt_ref, mod_ref, dsk_ref, wglu_ref, bglu_ref,
                pw_ref, psc_ref, pa_ref, pb_ref, wo_ref, nw2_ref, wq_ref,
                x1_ref, h2_ref, q_ref, *, d_model, tm):
    b = pl.program_id(0)
    gate1 = mod_ref[pl.ds(b, 1), 2 * d_model:3 * d_model]
    shift2 = mod_ref[pl.ds(b, 1), 3 * d_model:4 * d_model]
    scale2 = mod_ref[pl.ds(b, 1), 4 * d_model:5 * d_model]

    u_s = us_ref[0].astype(F32)
    y = yf_ref[0, 0] + yb_ref[0, 0] + dsk_ref[...] * u_s
    ge = _gelu_tanh(y)
    lin = jnp.dot(ge.astype(BF16), wglu_ref[...], preferred_element_type=F32) + bglu_ref[...]
    y_a = (ge * _sigmoid(lin)).astype(BF16)

    u_p = up_ref[0]
    ri = lax.broadcasted_iota(I32, (tm, tm), 0)
    ci = lax.broadcasted_iota(I32, (tm, tm), 1)
    same_row = (ri // GRID_W) == (ci // GRID_W)
    pos_r = ri % GRID_W
    pos_c = ci % GRID_W
    pos1 = lax.broadcasted_iota(I32, (tm, 1), 0) % GRID_W
    gw = pw_ref.shape[1]
    y_b_parts = []
    for gi, w in enumerate(POOL_WINDOWS):
        lo = jnp.maximum(pos_r - w // 2, 0)
        hi = jnp.minimum(pos_r + w // 2 - 1, GRID_W - 1)
        band = jnp.where(same_row & (pos_c >= lo) & (pos_c <= hi), 1.0, 0.0).astype(BF16)
        ug = u_p[:, gi * gw:(gi + 1) * gw]
        win = jnp.dot(band, ug, preferred_element_type=F32)
        cnt = (jnp.minimum(pos1 + w // 2 - 1, GRID_W - 1) - jnp.maximum(pos1 - w // 2, 0) + 1).astype(F32)
        p = win / cnt - ug.astype(F32)
        y_b_parts.append(jnp.dot(p.astype(BF16), pw_ref[gi], preferred_element_type=F32))
    y_b = (jnp.concatenate(y_b_parts, axis=-1) * psc_ref[...]).astype(BF16)

    g_a = _sigmoid(gt_ref[0, :, 0:d_model].astype(F32))
    g_b = _sigmoid(gt_ref[0, :, d_model:2 * d_model].astype(F32))
    m = (g_a * jnp.dot(y_a, pa_ref[...], preferred_element_type=F32)
         + g_b * jnp.dot(y_b, pb_ref[...], preferred_element_type=F32))
    x1 = x_ref[0] + gate1 * jnp.dot(m.astype(BF16), wo_ref[...], preferred_element_type=F32)
    x1_ref[0] = x1

    h2 = _modulated_norm(x1, nw2_ref[...], shift2, scale2).astype(BF16)
    h2_ref[0] = h2
    q_ref[0] = jnp.dot(h2, wq_ref[...], preferred_element_type=F32)


def _mix_call(x, y2, us, up, gates, mod, d_skip, w_glu, b_glu, pool_w, pool_scale,
              proj_a, proj_b, w_out, norm2_w, w_q, tm):
    bsz, seq, d_model = x.shape
    sw = us.shape[2]
    pwid = up.shape[2]
    assert seq % tm == 0 and tm % GRID_W == 0
    full = lambda a: pl.BlockSpec(a.shape, lambda b, i: (0,) * a.ndim)
    row = lambda a: a[None]
    tok = lambda wd: pl.BlockSpec((1, tm, wd), lambda b, i: (b, i, 0))
    args = (x, y2, y2, us, up, gates, mod, row(d_skip), w_glu.astype(BF16), row(b_glu),
            pool_w.astype(BF16), row(pool_scale), proj_a.astype(BF16), proj_b.astype(BF16),
            w_out.astype(BF16), row(norm2_w), w_q.astype(BF16))
    in_specs = [tok(d_model),
                pl.BlockSpec((1, 1, tm, sw), lambda b, i: (0, b, i, 0)),
                pl.BlockSpec((1, 1, tm, sw), lambda b, i: (1, b, i, 0)),
                tok(sw), tok(pwid), tok(2 * d_model)] + [full(a) for a in args[6:]]
    return pl.pallas_call(
        functools.partial(_mix_kernel, d_model=d_model, tm=tm),
        grid=(bsz, seq // tm),
        in_specs=in_specs,
        out_specs=[tok(d_model), tok(d_model), tok(d_model)],
        out_shape=[jax.ShapeDtypeStruct((bsz, seq, d_model), F32),
                   jax.ShapeDtypeStruct((bsz, seq, d_model), BF16),
                   jax.ShapeDtypeStruct((bsz, seq, d_model), F32)],
        compiler_params=_params(("arbitrary", "arbitrary")),
        name="mix",
    )(*args)


def _candidate_plan():
    plan = []
    for p in range(8):
        n = PEER_TOPK // (p + 1)
        for q0 in range(0, n, 8):
            plan.append((p, q0, min(8, n - q0)))
    return plan


def _route_kernel(q_ref, k1_ref, k2_ref, i1_ref, i2_ref, g_ref,
                  v1s, x1s, v2s, x2s, e1s, e2s, gws, *, tq):
    neg = -jnp.inf
    nk = PEER_KEYS
    kio = lax.broadcasted_iota(I32, (nk, tq), 0)
    sub8 = lax.broadcasted_iota(I32, (8, tq), 0)
    plan = _candidate_plan()
    n_cand = 8 * (len(plan) + 1)
    cio = lax.broadcasted_iota(I32, (n_cand, tq), 0)
    hd = q_ref.shape[1] // PEER_HEADS
    half = hd // 2

    def top16(s, vals_ref, idx_ref):
        for k in range(PEER_TOPK):
            m = jnp.max(s, axis=0, keepdims=True)
            i = jnp.min(jnp.where(s == m, kio, nk), axis=0, keepdims=True)
            vals_ref[k:k + 1, :] = m
            idx_ref[k:k + 1, :] = i
            s = jnp.where(kio == i, neg, s)

    for h in range(PEER_HEADS):
        q1 = q_ref[:, h * hd:h * hd + half]
        q2 = q_ref[:, h * hd + half:(h + 1) * hd]
        nt = (((1,), (1,)), ((), ()))
        s1 = lax.dot_general(k1_ref[...], q1, nt, precision=lax.Precision.HIGHEST,
                             preferred_element_type=F32)
        s2 = lax.dot_general(k2_ref[...], q2, nt, precision=lax.Precision.HIGHEST,
                             preferred_element_type=F32)
        top16(s1, v1s, x1s)
        top16(s2, v2s, x2s)

        cand, cid1, cid2 = [], [], []
        for (p, q0, n) in plan:
            piece = v1s[p:p + 1, :] + v2s[q0:q0 + 8, :]
            if n < 8:
                piece = jnp.where(sub8 < n, piece, neg)
            cand.append(piece)
            cid1.append(jnp.broadcast_to(x1s[p:p + 1, :], (8, tq)))
            cid2.append(x2s[q0:q0 + 8, :])
        cand.append(v1s[8:16, :] + v2s[0:1, :])
        cid1.append(x1s[8:16, :])
        cid2.append(jnp.broadcast_to(x2s[0:1, :], (8, tq)))
        cand = jnp.concatenate(cand, axis=0)
        cid1 = jnp.concatenate(cid1, axis=0)
        cid2 = jnp.concatenate(cid2, axis=0)

        tops = []
        for k in range(PEER_TOPK):
            m = jnp.max(cand, axis=0, keepdims=True)
            i = jnp.min(jnp.where(cand == m, cio, n_cand), axis=0, keepdims=True)
            hit = cio == i
            r = h * PEER_TOPK + k
            e1s[r:r + 1, :] = jnp.sum(jnp.where(hit, cid1, 0), axis=0, keepdims=True)
            e2s[r:r + 1, :] = jnp.sum(jnp.where(hit, cid2, 0), axis=0, keepdims=True)
            tops.append(m)
            cand = jnp.where(hit, neg, cand)
        ex = [jnp.exp(t - tops[0]) for t in tops]
        tot = ex[0]
        for e in ex[1:]:
            tot = tot + e
        for k in range(PEER_TOPK):
            r = h * PEER_TOPK + k
            gws[r:r + 1, :] = ex[k] / tot

    i1_ref[...] = e1s[...].T
    i2_ref[...] = e2s[...].T
    g_ref[...] = gws[...].T


def _route_call(q2d, k1, k2, tq):
    n_tok, d_q = q2d.shape
    ns = PEER_HEADS * PEER_TOPK
    assert n_tok % tq == 0 and ns == 128 and tq == 128
    tok = pl.BlockSpec((tq, ns), lambda i: (i, 0))
    return pl.pallas_call(
        functools.partial(_route_kernel, tq=tq),
        grid=(n_tok // tq,),
        in_specs=[pl.BlockSpec((tq, d_q), lambda i: (i, 0)),
                  pl.BlockSpec(k1.shape, lambda i: (0, 0)),
                  pl.BlockSpec(k2.shape, lambda i: (0, 0))],
        out_specs=[tok, tok, tok],
        out_shape=[jax.ShapeDtypeStruct((n_tok, ns), I32),
                   jax.ShapeDtypeStruct((n_tok, ns), I32),
                   jax.ShapeDtypeStruct((n_tok, ns), F32)],
        scratch_shapes=[pltpu.VMEM((PEER_TOPK, tq), F32), pltpu.VMEM((PEER_TOPK, tq), I32),
                        pltpu.VMEM((PEER_TOPK, tq), F32), pltpu.VMEM((PEER_TOPK, tq), I32),
                        pltpu.VMEM((ns, tq), I32), pltpu.VMEM((ns, tq), I32),
                        pltpu.VMEM((ns, tq), F32)],
        compiler_params=_params(("arbitrary",)),
        name="peer_route",
    )(q2d, k1, k2)


def _gbuild_kernel(i1_ref, i2_ref, g_ref, o_ref, *, tg, rpb):
    nk = PEER_KEYS
    sub = lax.broadcasted_iota(I32, (nk, i1_ref.shape[1]), 0)

    def body(tb, carry):
        t0 = pl.multiple_of(tb * 8, 8)
        i1_8 = i1_ref[pl.ds(t0, 8), :]
        i2_8 = i2_ref[pl.ds(t0, 8), :]
        gw_8 = g_ref[pl.ds(t0, 8), :]
        gh_8 = gw_8.astype(BF16).astype(F32)
        gl_8 = gw_8 - gh_8
        for u in range(8):
            hit1 = sub == i1_8[u:u + 1, :]
            a2 = jnp.concatenate([jnp.where(hit1, gh_8[u:u + 1, :], 0.0),
                                  jnp.where(hit1, gl_8[u:u + 1, :], 0.0)], axis=1).astype(BF16)
            b1 = jnp.where(sub == i2_8[u:u + 1, :], 1.0, 0.0)
            b2 = jnp.concatenate([b1, b1], axis=1).astype(BF16)
            gt = lax.dot_general(a2, b2, (((1,), (1,)), ((), ())), preferred_element_type=F32)
            row0 = pl.multiple_of((t0 + u) * rpb, rpb)
            for jb in range(nk // rpb):
                o_ref[0, jb, pl.ds(row0, rpb), :] = gt[jb * rpb:(jb + 1) * rpb, :]
        return carry

    lax.fori_loop(0, tg // 8, body, 0)


def _gbuild_call(i1, i2, g, tg, tm, rpb):
    n_tok, ns = i1.shape
    nk = PEER_KEYS
    assert n_tok % tm == 0 and tm % tg == 0 and nk % rpb == 0
    per = tm // tg
    tok = pl.BlockSpec((tg, ns), lambda i: (i, 0))
    return pl.pallas_call(
        functools.partial(_gbuild_kernel, tg=tg, rpb=rpb),
        grid=(n_tok // tg,),
        in_specs=[tok, tok, tok],
        out_specs=pl.BlockSpec((1, nk // rpb, tg * rpb, nk), lambda i: (i // per, 0, i % per, 0)),
        out_shape=jax.ShapeDtypeStruct((n_tok // tm, nk // rpb, tm * rpb, nk), F32),
        compiler_params=_params(("arbitrary",)),
        name="peer_gbuild",
    )(i1, i2, g)


def _peer_kernel(h_ref, ut_ref, v_ref, g_ref, x1_ref, mod_ref, fw_ref, o_ref,
                 *, d_model, rpb, tokens_per_batch, tm):
    i = pl.program_id(0)
    j = pl.program_id(1)

    @pl.when(j == 0)
    def _():
        o_ref[...] = jnp.zeros_like(o_ref)

    z = jnp.dot(h_ref[...], ut_ref[...], preferred_element_type=F32)
    gmat = jnp.concatenate([g_ref[0, 0, pl.ds(r, tm, stride=rpb), :] for r in range(rpb)], axis=1)
    w = (_gelu_tanh(z) * gmat).astype(BF16)
    o_ref[...] += jnp.dot(w, v_ref[...], preferred_element_type=F32)

    @pl.when(j == pl.num_programs(1) - 1)
    def _():
        b = (i * tm) // tokens_per_batch
        gate2 = mod_ref[pl.ds(b, 1), 5 * d_model:6 * d_model]
        x2 = x1_ref[...] + gate2 * o_ref[...]
        ms = jnp.mean(x2 * x2, axis=-1, keepdims=True)
        o_ref[...] = x2 * lax.rsqrt(ms + EPS) * fw_ref[...]


def _peer_call(h2, ut_bf, v_bf, gmat, x1, mod, final_w, tokens_per_batch, tm, rpb):
    n_tok, d_model = h2.shape
    n_exp = v_bf.shape[0]
    nk = PEER_KEYS
    eb = rpb * nk
    assert n_tok % tm == 0 and n_exp % eb == 0 and tokens_per_batch % tm == 0
    assert gmat.shape == (n_tok // tm, n_exp // eb, tm * rpb, nk)
    kern = functools.partial(_peer_kernel, d_model=d_model, rpb=rpb,
                             tokens_per_batch=tokens_per_batch, tm=tm)
    return pl.pallas_call(
        kern,
        grid=(n_tok // tm, n_exp // eb),
        in_specs=[pl.BlockSpec((tm, d_model), lambda i, j: (i, 0)),
                  pl.BlockSpec((d_model, eb), lambda i, j: (0, j)),
                  pl.BlockSpec((eb, d_model), lambda i, j: (j, 0)),
                  pl.BlockSpec((1, 1, tm * rpb, nk), lambda i, j: (i, j, 0, 0)),
                  pl.BlockSpec((tm, d_model), lambda i, j: (i, 0), pipeline_mode=pl.Buffered(1)),
                  pl.BlockSpec(mod.shape, lambda i, j: (0, 0)),
                  pl.BlockSpec((1, d_model), lambda i, j: (0, 0))],
        out_specs=pl.BlockSpec((tm, d_model), lambda i, j: (i, 0)),
        out_shape=jax.ShapeDtypeStruct((n_tok, d_model), F32),
        compiler_params=_params(("arbitrary", "arbitrary")),
        name="peer_experts",
    )(h2, ut_bf, v_bf, gmat, x1, mod, final_w[None])


def kernel(x, c, ctx, c_ctx, w_mod, b_mod, norm1_w, norm2_w, w_in, s5_a_re, s5_a_im, s5_log_dt,
           s5_b_re, s5_b_im, s5_c_re, s5_c_im, s5_d, w_glu, b_glu, pool_w, pool_scale, proj_a,
           proj_b, w_out, peer_wq, peer_k1, peer_k2, peer_u, peer_v, final_w):
    bsz, seq, d_model = x.shape
    depth = w_mod.shape[0]
    assert depth == 1, "only the single-layer configuration is implemented"
    l = 0
    s5_width = s5_d.shape[1]
    pool_width = pool_scale.shape[1]

    mod = _mod_call(c, c_ctx, w_mod[l], b_mod[l])

    w_in_bf = w_in[l].astype(BF16)
    tm = min(256, seq)
    us, up, gates = _inproj_call(x, mod, norm1_w[l], w_in_bf,
                                 (s5_width, pool_width, 2 * d_model), None, tm, "inproj_lat")
    lc = ctx.shape[1]
    (zc,) = _inproj_call(ctx, mod, norm1_w[l], w_in_bf[:, :s5_width], (s5_width,), bsz,
                         min(256, lc), "inproj_ctx")

    ab_re, ab_im, bb_re, bb_im = _disc_call(s5_a_re[l], s5_a_im[l], s5_log_dt[l], s5_b_re[l], s5_b_im[l])
    w_b, w_c, a_r, a_i = _s5_weights(ab_re, ab_im, bb_re, bb_im, s5_c_re[l], s5_c_im[l])
    chunk = min(256, lc, seq)
    y2 = _s5_call(zc, us, w_b, w_c, a_r, a_i, chunk)

    x1, h2, q = _mix_call(x, y2, us, up, gates, mod, s5_d[l], w_glu[l], b_glu[l], pool_w[l],
                          pool_scale[l], proj_a[l], proj_b[l], w_out[l], norm2_w[l], peer_wq[l], tm)

    n_tok = bsz * seq
    i1, i2, gw = _route_call(q.reshape(n_tok, d_model), peer_k1[l], peer_k2[l], 128)
    tm_peer, rpb = min(1024, seq), 4
    gmat = _gbuild_call(i1, i2, gw, 64, tm_peer, rpb)
    out = _peer_call(h2.reshape(n_tok, d_model), peer_u[l].astype(BF16).T, peer_v[l].astype(BF16),
                     gmat, x1.reshape(n_tok, d_model), mod, final_w, seq, tm_peer, rpb)
    return out.reshape(bsz, seq, d_model)
```

```python
import functools
import math

import jax
import jax.numpy as jnp
from jax import lax
from jax.experimental import pallas as pl
from jax.experimental.pallas import tpu as pltpu

F32 = jnp.float32
BF16 = jnp.bfloat16
I32 = jnp.int32

LANES = 128
EPS = 1e-6
GRID_W = 64
S5_GROUP = 16
S5_STATE = 64
S5_GROUPS_PER_BLOCK = 8
POOL_WINDOWS = (2, 4, 8, 16)
PEER_HEADS = 8
PEER_KEYS = 128
PEER_TOPK = 16
N_MOD = 6

V7X_VMEM_BYTES = 64 * 1024 * 1024
VMEM_LIMIT = 56 * 1024 * 1024


def _params(sem, vmem=VMEM_LIMIT):
    return pltpu.CompilerParams(dimension_semantics=sem, vmem_limit_bytes=vmem)


def _gelu_tanh(x):
    return 0.5 * x * (1.0 + jnp.tanh(math.sqrt(2.0 / math.pi) * (x + 0.044715 * (x * x * x))))


def _sigmoid(x):
    return 1.0 / (1.0 + jnp.exp(-x))


def _mod_kernel(cb_ref, w_ref, b_ref, o_ref, a_scr, *, n_rows):
    d_model, tn = w_ref.shape
    reps = tn // LANES

    @pl.when(pl.program_id(0) == 0)
    def _():
        cb = cb_ref[...]
        a_scr[...] = cb * _sigmoid(cb)

    def body(kb, accs):
        k0 = pl.multiple_of(kb * 8, 8)
        wt = w_ref[pl.ds(k0, 8), :]
        return tuple(acc + jnp.concatenate([a_scr[r, pl.ds(k0, 8), :]] * reps, axis=1) * wt
                     for r, acc in enumerate(accs))

    accs = lax.fori_loop(0, d_model // 8, body,
                         tuple(jnp.zeros((8, tn), F32) for _ in range(n_rows)), unroll=2)
    for r in range(8):
        if r < n_rows:
            o_ref[r:r + 1, :] = jnp.sum(accs[r], axis=0, keepdims=True) + b_ref[...]
        else:
            o_ref[r:r + 1, :] = jnp.zeros((1, tn), F32)


def _mod_call(c, c_ctx, w_mod, b_mod):
    bsz, d_model = c.shape
    n_out = w_mod.shape[1]
    n_rows = bsz + 1
    assert n_rows <= 8
    cb = jnp.broadcast_to(jnp.concatenate([c, c_ctx[None]], axis=0)[:, :, None], (n_rows, d_model, LANES))
    tn = 512
    return pl.pallas_call(
        functools.partial(_mod_kernel, n_rows=n_rows),
        grid=(n_out // tn,),
        in_specs=[pl.BlockSpec((n_rows, d_model, LANES), lambda j: (0, 0, 0)),
                  pl.BlockSpec((d_model, tn), lambda j: (0, j)),
                  pl.BlockSpec((1, tn), lambda j: (0, j))],
        out_specs=pl.BlockSpec((8, tn), lambda j: (0, j)),
        out_shape=jax.ShapeDtypeStruct((8, n_out), F32),
        scratch_shapes=[pltpu.VMEM((n_rows, d_model, LANES), F32)],
        compiler_params=_params(("arbitrary",)),
        name="mod",
    )(cb, w_mod, b_mod[None])


def _modulated_norm(x, nw, shift, scale):
    ms = jnp.mean(x * x, axis=-1, keepdims=True)
    y = x * lax.rsqrt(ms + EPS) * nw
    return y * (1.0 + scale) + shift


def _inproj_kernel(x_ref, mod_ref, nw_ref, w_ref, *o_refs, d_model, widths, ctx_row):
    row = ctx_row if ctx_row is not None else pl.program_id(0)
    shift = mod_ref[pl.ds(row, 1), 0:d_model]
    scale = mod_ref[pl.ds(row, 1), d_model:2 * d_model]
    h = _modulated_norm(x_ref[0], nw_ref[...], shift, scale).astype(BF16)
    c0 = 0
    for o_ref, wd in zip(o_refs, widths):
        o_ref[0] = jnp.dot(h, w_ref[:, c0:c0 + wd], preferred_element_type=F32).astype(o_ref.dtype)
        c0 += wd


def _inproj_call(x, mod, norm_w, w_bf, widths, ctx_row, tm, name):
    bsz, seq, d_model = x.shape
    n_in = w_bf.shape[1]
    assert sum(widths) == n_in and seq % tm == 0
    kern = functools.partial(_inproj_kernel, d_model=d_model, widths=widths, ctx_row=ctx_row)
    return pl.pallas_call(
        kern,
        grid=(bsz, seq // tm),
        in_specs=[pl.BlockSpec((1, tm, d_model), lambda b, i: (b, i, 0)),
                  pl.BlockSpec(mod.shape, lambda b, i: (0, 0)),
                  pl.BlockSpec((1, d_model), lambda b, i: (0, 0)),
                  pl.BlockSpec((d_model, n_in), lambda b, i: (0, 0))],
        out_specs=[pl.BlockSpec((1, tm, wd), lambda b, i: (b, i, 0)) for wd in widths],
        out_shape=[jax.ShapeDtypeStruct((bsz, seq, wd), BF16) for wd in widths],
        compiler_params=_params(("arbitrary", "arbitrary")),
        name=name,
    )(x, mod, norm_w[None], w_bf)


def _disc_kernel(are_ref, aim_ref, ldt_ref, bre_ref, bim_ref, abr_ref, abi_ref, bbr_ref, bbi_ref):
    a_re = are_ref[...]
    a_im = aim_ref[...]
    dt = jnp.exp(ldt_ref[...])
    mag = jnp.exp(dt * a_re)
    ab_re = mag * jnp.cos(dt * a_im)
    ab_im = mag * jnp.sin(dt * a_im)
    den = a_re * a_re + a_im * a_im
    nr = ab_re - 1.0
    ni = ab_im
    f_re = (nr * a_re + ni * a_im) / den
    f_im = (ni * a_re - nr * a_im) / den
    b_re = bre_ref[...]
    b_im = bim_ref[...]
    abr_ref[...] = ab_re
    abi_ref[...] = ab_im
    bbr_ref[...] = f_re * b_re - f_im * b_im
    bbi_ref[...] = f_re * b_im + f_im * b_re


def _disc_call(a_re, a_im, log_dt, b_re, b_im):
    nd, g, p, h = b_re.shape
    rows, cols = nd * g, p * h
    rep = lambda a: jnp.broadcast_to(a[..., None], (nd, g, p, h)).reshape(rows, cols)
    args = (rep(a_re), rep(a_im),
            jnp.broadcast_to(log_dt[..., None, None], (nd, g, p, h)).reshape(rows, cols),
            b_re.reshape(rows, cols), b_im.reshape(rows, cols))
    outs = pl.pallas_call(
        _disc_kernel,
        out_shape=[jax.ShapeDtypeStruct((rows, cols), F32)] * 4,
        name="s5_disc",
    )(*args)
    ab_re, ab_im, bb_re, bb_im = (o.reshape(nd, g, p, h) for o in outs)
    return ab_re[..., 0], ab_im[..., 0], bb_re, bb_im


def _s5_weights(ab_re, ab_im, bb_re, bb_im, c_re, c_im):
    nd, g, p, h = bb_re.shape
    gb = S5_GROUPS_PER_BLOCK
    nb = g // gb
    eye = jnp.eye(gb, dtype=bool)

    def in_mat(bb):
        t = bb.reshape(nd, nb, gb, p, h).transpose(0, 1, 2, 4, 3)
        t = jnp.where(eye[None, None, :, None, :, None], t[:, :, :, :, None, :], 0.0)
        return t.reshape(nd, nb, gb * h, gb * p)

    def out_mat(cc):
        t = cc.reshape(nd, nb, gb, h, p).transpose(0, 1, 2, 4, 3)
        t = jnp.where(eye[None, None, :, None, :, None], t[:, :, :, :, None, :], 0.0)
        return t.reshape(nd, nb, gb * p, gb * h)

    w_b = jnp.concatenate([in_mat(bb_re), in_mat(bb_im)], axis=-1).astype(BF16)
    w_c = jnp.concatenate([out_mat(c_re.astype(F32)), -out_mat(c_im.astype(F32))], axis=-2).astype(BF16)
    a_r = ab_re.reshape(nd, nb, gb * p)
    a_i = ab_im.reshape(nd, nb, gb * p)
    return w_b, w_c, a_r, a_i


def _s5_kernel(zc_ref, zl_ref, wb_ref, wc_ref, ar_ref, ai_ref, y_ref, sbuf, h_ref,
               *, n_ctx_chunks, chunk, nb, half, nbat):
    d = pl.program_id(1)
    c = pl.program_id(2)
    cw = 2 * half
    bw = wb_ref.shape[2]

    @pl.when(c == 0)
    def _():
        h_ref[...] = jnp.zeros_like(h_ref)

    nl = cw // LANES
    hl = half // LANES

    def project(src_ref):
        for e in range(nbat):
            for k in range(nb):
                xk = src_ref[e, :, k * bw:(k + 1) * bw]
                bu = jnp.dot(xk, wb_ref[0, k], preferred_element_type=F32)
                for l in range(nl):
                    sbuf[e, l, pl.ds(k, chunk, stride=nb), :] = bu[:, l * LANES:(l + 1) * LANES]

    def scan():
        a_r = [ar_ref[0, :, l * LANES:(l + 1) * LANES] for l in range(hl)]
        a_i = [ai_ref[0, :, l * LANES:(l + 1) * LANES] for l in range(hl)]

        def step(s, carry):
            t = s + d * (chunk - 1 - 2 * s)
            row = pl.multiple_of(t * nb, nb)
            out = []
            for e in range(nbat):
                for l in range(hl):
                    h_r, h_i = carry[2 * (e * hl + l)], carry[2 * (e * hl + l) + 1]
                    n_r = a_r[l] * h_r - a_i[l] * h_i + sbuf[e, l, pl.ds(row, nb), :]
                    n_i = a_r[l] * h_i + a_i[l] * h_r + sbuf[e, hl + l, pl.ds(row, nb), :]
                    sbuf[e, l, pl.ds(row, nb), :] = n_r
                    sbuf[e, hl + l, pl.ds(row, nb), :] = n_i
                    out += [n_r, n_i]
            return tuple(out)

        init = []
        for e in range(nbat):
            for l in range(hl):
                init += [h_ref[e, l], h_ref[e, hl + l]]
        fin = lax.fori_loop(0, chunk, step, tuple(init), unroll=2)
        for e in range(nbat):
            for l in range(hl):
                h_ref[e, l] = fin[2 * (e * hl + l)]
                h_ref[e, hl + l] = fin[2 * (e * hl + l) + 1]

    def readout():
        for e in range(nbat):
            for k in range(nb):
                hk = jnp.concatenate([sbuf[e, l, pl.ds(k, chunk, stride=nb), :] for l in range(nl)], axis=1)
                y_ref[0, e, :, k * bw:(k + 1) * bw] = jnp.dot(hk.astype(BF16), wc_ref[0, k],
                                                              preferred_element_type=F32)

    @pl.when(c < n_ctx_chunks)
    def _():
        project(zc_ref)
        scan()

    @pl.when(c >= n_ctx_chunks)
    def _():
        project(zl_ref)
        scan()
        readout()


def _s5_call(zc, zl, w_b, w_c, a_r, a_i, chunk):
    bsz, lc, width = zc.shape
    seq = zl.shape[1]
    nd, nb, bw, cw = w_b.shape
    half = cw // 2
    assert nb == 8 and lc % chunk == 0 and seq % chunk == 0 and width == nb * bw
    ncc, nlc = lc // chunk, seq // chunk
    nbat = 2 if bsz % 2 == 0 else 1

    def ctx_idx(b, d, c):
        cc = jnp.minimum(c, ncc - 1)
        return (b, cc + d * (ncc - 1 - 2 * cc), 0)

    def lat_chunk(d, c):
        lc_ = jnp.maximum(c - ncc, 0)
        return lc_ + d * (nlc - 1 - 2 * lc_)

    kern = functools.partial(_s5_kernel, n_ctx_chunks=ncc, chunk=chunk, nb=nb, half=half, nbat=nbat)
    return pl.pallas_call(
        kern,
        grid=(bsz // nbat, nd, ncc + nlc),
        in_specs=[pl.BlockSpec((nbat, chunk, width), ctx_idx),
                  pl.BlockSpec((nbat, chunk, width), lambda b, d, c: (b, lat_chunk(d, c), 0)),
                  pl.BlockSpec((1, nb, bw, cw), lambda b, d, c: (d, 0, 0, 0)),
                  pl.BlockSpec((1, nb, cw, bw), lambda b, d, c: (d, 0, 0, 0)),
                  pl.BlockSpec((1, nb, half), lambda b, d, c: (d, 0, 0)),
                  pl.BlockSpec((1, nb, half), lambda b, d, c: (d, 0, 0))],
        out_specs=pl.BlockSpec((1, nbat, chunk, width), lambda b, d, c: (d, b, lat_chunk(d, c), 0)),
        out_shape=jax.ShapeDtypeStruct((nd, bsz, seq, width), F32),
        scratch_shapes=[pltpu.VMEM((nbat, cw // LANES, chunk * nb, LANES), F32),
                        pltpu.VMEM((nbat, cw // LANES, nb, LANES), F32)],
        compiler_params=_params(("arbitrary", "arbitrary", "arbitrary")),
        name="s5_scan",
    )(zc, zl, w_b, w_c, a_r, a_i)


def _mix_kernel(x_ref, yf_ref, yb_ref, us_ref, up_ref, gt_ref, mod_ref, dsk_ref, wglu_ref, bglu_ref,
                pw_ref, psc_ref, pa_ref, pb_ref, wo_ref, nw2_ref, wq_ref,
                x1_ref, h2_ref, q_ref, *, d_model, tm):
    b = pl.program_id(0)
    gate1 = mod_ref[pl.ds(b, 1), 2 * d_model:3 * d_model]
    shift2 = mod_ref[pl.ds(b, 1), 3 * d_model:4 * d_model]
    scale2 = mod_ref[pl.ds(b, 1), 4 * d_model:5 * d_model]

    u_s = us_ref[0].astype(F32)
    y = yf_ref[0, 0] + yb_ref[0, 0] + dsk_ref[...] * u_s
    ge = _gelu_tanh(y)
    lin = jnp.dot(ge.astype(BF16), wglu_ref[...], preferred_element_type=F32) + bglu_ref[...]
    y_a = (ge * _sigmoid(lin)).astype(BF16)

    u_p = up_ref[0]
    ri = lax.broadcasted_iota(I32, (tm, tm), 0)
    ci = lax.broadcasted_iota(I32, (tm, tm), 1)
    same_row = (ri // GRID_W) == (ci // GRID_W)
    pos_r = ri % GRID_W
    pos_c = ci % GRID_W
    pos1 = lax.broadcasted_iota(I32, (tm, 1), 0) % GRID_W
    gw = pw_ref.shape[1]
    y_b_parts = []
    for gi, w in enumerate(POOL_WINDOWS):
        lo = jnp.maximum(pos_r - w // 2, 0)
        hi = jnp.minimum(pos_r + w // 2 - 1, GRID_W - 1)
        band = jnp.where(same_row & (pos_c >= lo) & (pos_c <= hi), 1.0, 0.0).astype(BF16)
        ug = u_p[:, gi * gw:(gi + 1) * gw]
        win = jnp.dot(band, ug, preferred_element_type=F32)
        cnt = (jnp.minimum(pos1 + w // 2 - 1, GRID_W - 1) - jnp.maximum(pos1 - w // 2, 0) + 1).astype(F32)
        p = win / cnt - ug.astype(F32)
        y_b_parts.append(jnp.dot(p.astype(BF16), pw_ref[gi], preferred_element_type=F32))
    y_b = (jnp.concatenate(y_b_parts, axis=-1) * psc_ref[...]).astype(BF16)

    g_a = _sigmoid(gt_ref[0, :, 0:d_model].astype(F32))
    g_b = _sigmoid(gt_ref[0, :, d_model:2 * d_model].astype(F32))
    m = (g_a * jnp.dot(y_a, pa_ref[...], preferred_element_type=F32)
         + g_b * jnp.dot(y_b, pb_ref[...], preferred_element_type=F32))
    x1 = x_ref[0] + gate1 * jnp.dot(m.astype(BF16), wo_ref[...], preferred_element_type=F32)
    x1_ref[0] = x1

    h2 = _modulated_norm(x1, nw2_ref[...], shift2, scale2).astype(BF16)
    h2_ref[0] = h2
    q_ref[0] = jnp.dot(h2, wq_ref[...], preferred_element_type=F32)


def _mix_call(x, y2, us, up, gates, mod, d_skip, w_glu, b_glu, pool_w, pool_scale,
              proj_a, proj_b, w_out, norm2_w, w_q, tm):
    bsz, seq, d_model = x.shape
    sw = us.shape[2]
    pwid = up.shape[2]
    assert seq % tm == 0 and tm % GRID_W == 0
    full = lambda a: pl.BlockSpec(a.shape, lambda b, i: (0,) * a.ndim)
    row = lambda a: a[None]
    tok = lambda wd: pl.BlockSpec((1, tm, wd), lambda b, i: (b, i, 0))
    args = (x, y2, y2, us, up, gates, mod, row(d_skip), w_glu.astype(BF16), row(b_glu),
            pool_w.astype(BF16), row(pool_scale), proj_a.astype(BF16), proj_b.astype(BF16),
            w_out.astype(BF16), row(norm2_w), w_q.astype(BF16))
    in_specs = [tok(d_model),
                pl.BlockSpec((1, 1, tm, sw), lambda b, i: (0, b, i, 0)),
                pl.BlockSpec((1, 1, tm, sw), lambda b, i: (1, b, i, 0)),
                tok(sw), tok(pwid), tok(2 * d_model)] + [full(a) for a in args[6:]]
    return pl.pallas_call(
        functools.partial(_mix_kernel, d_model=d_model, tm=tm),
        grid=(bsz, seq // tm),
        in_specs=in_specs,
        out_specs=[tok(d_model), tok(d_model), tok(d_model)],
        out_shape=[jax.ShapeDtypeStruct((bsz, seq, d_model), F32),
                   jax.ShapeDtypeStruct((bsz, seq, d_model), BF16),
                   jax.ShapeDtypeStruct((bsz, seq, d_model), F32)],
        compiler_params=_params(("arbitrary", "arbitrary")),
        name="mix",
    )(*args)


def _candidate_plan():
    plan = []
    for p in range(8):
        n = PEER_TOPK // (p + 1)
        for q0 in range(0, n, 8):
            plan.append((p, q0, min(8, n - q0)))
    return plan


def _route_kernel(q_ref, k1_ref, k2_ref, o_ref,
                  v1s, x1s, v2s, x2s, e1s, e2s, gws, t1s, t2s, tgs, *, tq, rpb):
    neg = -jnp.inf
    nk = PEER_KEYS
    ns = PEER_HEADS * PEER_TOPK
    kio = lax.broadcasted_iota(I32, (nk, tq), 0).astype(F32)
    sub8 = lax.broadcasted_iota(I32, (8, tq), 0)
    plan = _candidate_plan()
    n_cand = 8 * (len(plan) + 1)
    cio = lax.broadcasted_iota(I32, (n_cand, tq), 0).astype(F32)
    sub_s = lax.broadcasted_iota(I32, (nk, ns), 0).astype(F32)
    hd = q_ref.shape[1] // PEER_HEADS
    half = hd // 2
    tokens_per_head = tq // PEER_HEADS

    @pl.when(pl.program_id(0) == 0)
    def _():
        t1s[...] = jnp.zeros_like(t1s)
        t2s[...] = jnp.zeros_like(t2s)
        tgs[...] = jnp.zeros_like(tgs)

    def top16(s, vals_ref, idx_ref):
        for k in range(PEER_TOPK):
            m = jnp.max(s, axis=0, keepdims=True)
            i = jnp.min(jnp.where(s == m, kio, float(nk)), axis=0, keepdims=True)
            vals_ref[k:k + 1, :] = m
            idx_ref[k:k + 1, :] = i
            s = jnp.where(kio == i, neg, s)

    def route_head(h):
        q1 = q_ref[:, h * hd:h * hd + half]
        q2 = q_ref[:, h * hd + half:(h + 1) * hd]
        nt = (((1,), (1,)), ((), ()))
        s1 = lax.dot_general(k1_ref[...], q1, nt, precision=lax.Precision.HIGHEST,
                             preferred_element_type=F32)
        s2 = lax.dot_general(k2_ref[...], q2, nt, precision=lax.Precision.HIGHEST,
                             preferred_element_type=F32)
        top16(s1, v1s, x1s)
        top16(s2, v2s, x2s)

        cand, cid1, cid2 = [], [], []
        for (p, q0, n) in plan:
            piece = v1s[p:p + 1, :] + v2s[q0:q0 + 8, :]
            if n < 8:
                piece = jnp.where(sub8 < n, piece, neg)
            cand.append(piece)
            cid1.append(jnp.broadcast_to(x1s[p:p + 1, :], (8, tq)))
            cid2.append(x2s[q0:q0 + 8, :])
        cand.append(v1s[8:16, :] + v2s[0:1, :])
        cid1.append(x1s[8:16, :])
        cid2.append(jnp.broadcast_to(x2s[0:1, :], (8, tq)))
        cand = jnp.concatenate(cand, axis=0)
        cid1 = jnp.concatenate(cid1, axis=0)
        cid2 = jnp.concatenate(cid2, axis=0)

        tops = []
        for k in range(PEER_TOPK):
            m = jnp.max(cand, axis=0, keepdims=True)
            i = jnp.min(jnp.where(cand == m, cio, float(n_cand)), axis=0, keepdims=True)
            hit = cio == i
            r = h * PEER_TOPK + k
            e1s[r:r + 1, :] = jnp.sum(jnp.where(hit, cid1, 0.0), axis=0, keepdims=True)
            e2s[r:r + 1, :] = jnp.sum(jnp.where(hit, cid2, 0.0), axis=0, keepdims=True)
            tops.append(m)
            cand = jnp.where(hit, neg, cand)
        ex = [jnp.exp(t - tops[0]) for t in tops]
        tot = ex[0]
        for e in ex[1:]:
            tot = tot + e
        for k in range(PEER_TOPK):
            r = h * PEER_TOPK + k
            gws[r:r + 1, :] = ex[k] / tot

    def expand_tokens(t0):
        i1_8 = t1s[t0:t0 + 8, :]
        i2_8 = t2s[t0:t0 + 8, :]
        gw_8 = tgs[t0:t0 + 8, :]
        gh_8 = gw_8.astype(BF16).astype(F32)
        gl_8 = gw_8 - gh_8
        for u in range(8):
            hit1 = sub_s == i1_8[u:u + 1, :]
            a2 = jnp.concatenate([jnp.where(hit1, gh_8[u:u + 1, :], 0.0),
                                  jnp.where(hit1, gl_8[u:u + 1, :], 0.0)], axis=1).astype(BF16)
            b1 = jnp.where(sub_s == i2_8[u:u + 1, :], 1.0, 0.0)
            b2 = jnp.concatenate([b1, b1], axis=1).astype(BF16)
            gt = lax.dot_general(a2, b2, (((1,), (1,)), ((), ())), preferred_element_type=F32)
            row0 = (t0 + u) * rpb
            for jb in range(nk // rpb):
                o_ref[0, jb, row0:row0 + rpb, :] = gt[jb * rpb:(jb + 1) * rpb, :]

    for h in range(PEER_HEADS):
        route_head(h)
        for t0 in range(h * tokens_per_head, (h + 1) * tokens_per_head, 8):
            expand_tokens(t0)

    t1s[...] = e1s[...].T
    t2s[...] = e2s[...].T
    tgs[...] = gws[...].T


def _route_call(q2d, k1, k2, tq, tm, rpb):
    n_tok, d_q = q2d.shape
    nk = PEER_KEYS
    ns = PEER_HEADS * PEER_TOPK
    assert n_tok % tm == 0 and tm % tq == 0 and ns == nk and tq == LANES and nk % rpb == 0
    n_tiles = n_tok // tq
    per = tm // tq

    def out_idx(i):
        p = jnp.maximum(i - 1, 0)
        return (p // per, 0, p % per, 0)

    return pl.pallas_call(
        functools.partial(_route_kernel, tq=tq, rpb=rpb),
        grid=(n_tiles + 1,),
        in_specs=[pl.BlockSpec((tq, d_q), lambda i: (jnp.minimum(i, n_tiles - 1), 0)),
                  pl.BlockSpec(k1.shape, lambda i: (0, 0)),
                  pl.BlockSpec(k2.shape, lambda i: (0, 0))],
        out_specs=pl.BlockSpec((1, nk // rpb, tq * rpb, nk), out_idx),
        out_shape=jax.ShapeDtypeStruct((n_tok // tm, nk // rpb, tm * rpb, nk), F32),
        scratch_shapes=[pltpu.VMEM((PEER_TOPK, tq), F32), pltpu.VMEM((PEER_TOPK, tq), F32),
                        pltpu.VMEM((PEER_TOPK, tq), F32), pltpu.VMEM((PEER_TOPK, tq), F32),
                        pltpu.VMEM((ns, tq), F32), pltpu.VMEM((ns, tq), F32), pltpu.VMEM((ns, tq), F32),
                        pltpu.VMEM((tq, ns), F32), pltpu.VMEM((tq, ns), F32), pltpu.VMEM((tq, ns), F32)],
        compiler_params=_params(("arbitrary",)),
        name="peer_route",
    )(q2d, k1, k2)


def _peer_kernel(h_ref, u_ref, v_ref, g_ref, x1_ref, mod_ref, fw_ref, o_ref,
                 *, d_model, rpb, tokens_per_batch, tm):
    i = pl.program_id(0)
    j = pl.program_id(1)

    @pl.when(j == 0)
    def _():
        o_ref[...] = jnp.zeros_like(o_ref)

    z = lax.dot_general(h_ref[...], u_ref[...], (((1,), (1,)), ((), ())), preferred_element_type=F32)
    gmat = jnp.concatenate([g_ref[0, 0, pl.ds(r, tm, stride=rpb), :] for r in range(rpb)], axis=1)
    w = (_gelu_tanh(z) * gmat).astype(BF16)
    o_ref[...] += jnp.dot(w, v_ref[...], preferred_element_type=F32)

    @pl.when(j == pl.num_programs(1) - 1)
    def _():
        b = (i * tm) // tokens_per_batch
        gate2 = mod_ref[pl.ds(b, 1), 5 * d_model:6 * d_model]
        x2 = x1_ref[...] + gate2 * o_ref[...]
        ms = jnp.mean(x2 * x2, axis=-1, keepdims=True)
        o_ref[...] = x2 * lax.rsqrt(ms + EPS) * fw_ref[...]


def _peer_call(h2, u_bf, v_bf, gmat, x1, mod, final_w, tokens_per_batch, tm, rpb):
    n_tok, d_model = h2.shape
    n_exp = v_bf.shape[0]
    nk = PEER_KEYS
    eb = rpb * nk
    assert n_tok % tm == 0 and n_exp % eb == 0 and tokens_per_batch % tm == 0
    assert gmat.shape == (n_tok // tm, n_exp // eb, tm * rpb, nk)
    kern = functools.partial(_peer_kernel, d_model=d_model, rpb=rpb,
                             tokens_per_batch=tokens_per_batch, tm=tm)
    return pl.pallas_call(
        kern,
        grid=(n_tok // tm, n_exp // eb),
        in_specs=[pl.BlockSpec((tm, d_model), lambda i, j: (i, 0)),
                  pl.BlockSpec((eb, d_model), lambda i, j: (j, 0)),
                  pl.BlockSpec((eb, d_model), lambda i, j: (j, 0)),
                  pl.BlockSpec((1, 1, tm * rpb, nk), lambda i, j: (i, j, 0, 0)),
                  pl.BlockSpec((tm, d_model), lambda i, j: (i, 0), pipeline_mode=pl.Buffered(1)),
                  pl.BlockSpec(mod.shape, lambda i, j: (0, 0)),
                  pl.BlockSpec((1, d_model), lambda i, j: (0, 0))],
        out_specs=pl.BlockSpec((tm, d_model), lambda i, j: (i, 0)),
        out_shape=jax.ShapeDtypeStruct((n_tok, d_model), F32),
        compiler_params=_params(("arbitrary", "arbitrary")),
        name="peer_experts",
    )(h2, u_bf, v_bf, gmat, x1, mod, final_w[None])


def kernel(x, c, ctx, c_ctx, w_mod, b_mod, norm1_w, norm2_w, w_in, s5_a_re, s5_a_im, s5_log_dt,
           s5_b_re, s5_b_im, s5_c_re, s5_c_im, s5_d, w_glu, b_glu, pool_w, pool_scale, proj_a,
           proj_b, w_out, peer_wq, peer_k1, peer_k2, peer_u, peer_v, final_w):
    bsz, seq, d_model = x.shape
    depth = w_mod.shape[0]
    assert depth == 1, "only the single-layer configuration is implemented"
    l = 0
    s5_width = s5_d.shape[1]
    pool_width = pool_scale.shape[1]

    mod = _mod_call(c, c_ctx, w_mod[l], b_mod[l])

    w_in_bf = w_in[l].astype(BF16)
    tm = min(256, seq)
    us, up, gates = _inproj_call(x, mod, norm1_w[l], w_in_bf,
                                 (s5_width, pool_width, 2 * d_model), None, tm, "inproj_lat")
    lc = ctx.shape[1]
    (zc,) = _inproj_call(ctx, mod, norm1_w[l], w_in_bf[:, :s5_width], (s5_width,), bsz,
                         min(256, lc), "inproj_ctx")

    ab_re, ab_im, bb_re, bb_im = _disc_call(s5_a_re[l], s5_a_im[l], s5_log_dt[l], s5_b_re[l], s5_b_im[l])
    w_b, w_c, a_r, a_i = _s5_weights(ab_re, ab_im, bb_re, bb_im, s5_c_re[l], s5_c_im[l])
    chunk = min(256, lc, seq)
    y2 = _s5_call(zc, us, w_b, w_c, a_r, a_i, chunk)

    x1, h2, q = _mix_call(x, y2, us, up, gates, mod, s5_d[l], w_glu[l], b_glu[l], pool_w[l],
                          pool_scale[l], proj_a[l], proj_b[l], w_out[l], norm2_w[l], peer_wq[l], tm)

    n_tok = bsz * seq
    tm_peer, rpb = min(1024, seq), 4
    gmat = _route_call(q.reshape(n_tok, d_model), peer_k1[l], peer_k2[l], LANES, tm_peer, rpb)
    out = _peer_call(h2.reshape(n_tok, d_model), peer_u[l].astype(BF16), peer_v[l].astype(BF16),
                     gmat, x1.reshape(n_tok, d_model), mod, final_w, seq, tm_peer, rpb)
    return out.reshape(bsz, seq, d_model)
```

```python
import functools
import math

import jax
import jax.numpy as jnp
from jax import lax
from jax.experimental import pallas as pl
from jax.experimental.pallas import tpu as pltpu

F32 = jnp.float32
BF16 = jnp.bfloat16
I32 = jnp.int32

LANES = 128
EPS = 1e-6
GRID_W = 64
S5_GROUP = 16
S5_STATE = 64
S5_GROUPS_PER_BLOCK = 8
POOL_WINDOWS = (2, 4, 8, 16)
PEER_HEADS = 8
PEER_KEYS = 128
PEER_TOPK = 16
N_MOD = 6

V7X_VMEM_BYTES = 64 * 1024 * 1024
VMEM_LIMIT = 56 * 1024 * 1024


def _params(sem, vmem=VMEM_LIMIT):
    return pltpu.CompilerParams(dimension_semantics=sem, vmem_limit_bytes=vmem)


def _gelu_tanh(x):
    return 0.5 * x * (1.0 + jnp.tanh(math.sqrt(2.0 / math.pi) * (x + 0.044715 * (x * x * x))))


def _sigmoid(x):
    return 1.0 / (1.0 + jnp.exp(-x))


def _mod_kernel(cb_ref, w_ref, b_ref, o_ref, a_scr, *, n_rows):
    d_model, tn = w_ref.shape
    reps = tn // LANES

    @pl.when(pl.program_id(0) == 0)
    def _():
        cb = cb_ref[...]
        a_scr[...] = cb * _sigmoid(cb)

    def body(kb, accs):
        k0 = pl.multiple_of(kb * 8, 8)
        wt = w_ref[pl.ds(k0, 8), :]
        return tuple(acc + jnp.concatenate([a_scr[r, pl.ds(k0, 8), :]] * reps, axis=1) * wt
                     for r, acc in enumerate(accs))

    accs = lax.fori_loop(0, d_model // 8, body,
                         tuple(jnp.zeros((8, tn), F32) for _ in range(n_rows)), unroll=2)
    for r in range(8):
        if r < n_rows:
            o_ref[r:r + 1, :] = jnp.sum(accs[r], axis=0, keepdims=True) + b_ref[...]
        else:
            o_ref[r:r + 1, :] = jnp.zeros((1, tn), F32)


def _mod_call(c, c_ctx, w_mod, b_mod):
    bsz, d_model = c.shape
    n_out = w_mod.shape[1]
    n_rows = bsz + 1
    assert n_rows <= 8
    cb = jnp.broadcast_to(jnp.concatenate([c, c_ctx[None]], axis=0)[:, :, None], (n_rows, d_model, LANES))
    tn = 512
    return pl.pallas_call(
        functools.partial(_mod_kernel, n_rows=n_rows),
        grid=(n_out // tn,),
        in_specs=[pl.BlockSpec((n_rows, d_model, LANES), lambda j: (0, 0, 0)),
                  pl.BlockSpec((d_model, tn), lambda j: (0, j)),
                  pl.BlockSpec((1, tn), lambda j: (0, j))],
        out_specs=pl.BlockSpec((8, tn), lambda j: (0, j)),
        out_shape=jax.ShapeDtypeStruct((8, n_out), F32),
        scratch_shapes=[pltpu.VMEM((n_rows, d_model, LANES), F32)],
        compiler_params=_params(("arbitrary",)),
        name="mod",
    )(cb, w_mod, b_mod[None])


def _modulated_norm(x, nw, shift, scale):
    ms = jnp.mean(x * x, axis=-1, keepdims=True)
    y = x * lax.rsqrt(ms + EPS) * nw
    return y * (1.0 + scale) + shift


def _inproj_kernel(x_ref, mod_ref, nw_ref, w_ref, *o_refs, d_model, widths, ctx_row):
    row = ctx_row if ctx_row is not None else pl.program_id(0)
    shift = mod_ref[pl.ds(row, 1), 0:d_model]
    scale = mod_ref[pl.ds(row, 1), d_model:2 * d_model]
    h = _modulated_norm(x_ref[0], nw_ref[...], shift, scale).astype(BF16)
    c0 = 0
    for o_ref, wd in zip(o_refs, widths):
        o_ref[0] = jnp.dot(h, w_ref[:, c0:c0 + wd], preferred_element_type=F32).astype(o_ref.dtype)
        c0 += wd


def _inproj_call(x, mod, norm_w, w_bf, widths, ctx_row, tm, name):
    bsz, seq, d_model = x.shape
    n_in = w_bf.shape[1]
    assert sum(widths) == n_in and seq % tm == 0
    kern = functools.partial(_inproj_kernel, d_model=d_model, widths=widths, ctx_row=ctx_row)
    return pl.pallas_call(
        kern,
        grid=(bsz, seq // tm),
        in_specs=[pl.BlockSpec((1, tm, d_model), lambda b, i: (b, i, 0)),
                  pl.BlockSpec(mod.shape, lambda b, i: (0, 0)),
                  pl.BlockSpec((1, d_model), lambda b, i: (0, 0)),
                  pl.BlockSpec((d_model, n_in), lambda b, i: (0, 0))],
        out_specs=[pl.BlockSpec((1, tm, wd), lambda b, i: (b, i, 0)) for wd in widths],
        out_shape=[jax.ShapeDtypeStruct((bsz, seq, wd), BF16) for wd in widths],
        compiler_params=_params(("arbitrary", "arbitrary")),
        name=name,
    )(x, mod, norm_w[None], w_bf)


def _disc_kernel(are_ref, aim_ref, ldt_ref, bre_ref, bim_ref, abr_ref, abi_ref, bbr_ref, bbi_ref):
    a_re = are_ref[...]
    a_im = aim_ref[...]
    dt = jnp.exp(ldt_ref[...])
    mag = jnp.exp(dt * a_re)
    ab_re = mag * jnp.cos(dt * a_im)
    ab_im = mag * jnp.sin(dt * a_im)
    den = a_re * a_re + a_im * a_im
    nr = ab_re - 1.0
    ni = ab_im
    f_re = (nr * a_re + ni * a_im) / den
    f_im = (ni * a_re - nr * a_im) / den
    b_re = bre_ref[...]
    b_im = bim_ref[...]
    abr_ref[...] = ab_re
    abi_ref[...] = ab_im
    bbr_ref[...] = f_re * b_re - f_im * b_im
    bbi_ref[...] = f_re * b_im + f_im * b_re


def _disc_call(a_re, a_im, log_dt, b_re, b_im):
    nd, g, p, h = b_re.shape
    rows, cols = nd * g, p * h
    rep = lambda a: jnp.broadcast_to(a[..., None], (nd, g, p, h)).reshape(rows, cols)
    args = (rep(a_re), rep(a_im),
            jnp.broadcast_to(log_dt[..., None, None], (nd, g, p, h)).reshape(rows, cols),
            b_re.reshape(rows, cols), b_im.reshape(rows, cols))
    outs = pl.pallas_call(
        _disc_kernel,
        out_shape=[jax.ShapeDtypeStruct((rows, cols), F32)] * 4,
        name="s5_disc",
    )(*args)
    ab_re, ab_im, bb_re, bb_im = (o.reshape(nd, g, p, h) for o in outs)
    return ab_re[..., 0], ab_im[..., 0], bb_re, bb_im


def _s5_weights(ab_re, ab_im, bb_re, bb_im, c_re, c_im):
    nd, g, p, h = bb_re.shape
    gb = S5_GROUPS_PER_BLOCK
    nb = g // gb
    eye = jnp.eye(gb, dtype=bool)

    def in_mat(bb):
        t = bb.reshape(nd, nb, gb, p, h).transpose(0, 1, 2, 4, 3)
        t = jnp.where(eye[None, None, :, None, :, None], t[:, :, :, :, None, :], 0.0)
        return t.reshape(nd, nb, gb * h, gb * p)

    def out_mat(cc):
        t = cc.reshape(nd, nb, gb, h, p).transpose(0, 1, 2, 4, 3)
        t = jnp.where(eye[None, None, :, None, :, None], t[:, :, :, :, None, :], 0.0)
        return t.reshape(nd, nb, gb * p, gb * h)

    w_b = jnp.concatenate([in_mat(bb_re), in_mat(bb_im)], axis=-1).astype(BF16)
    w_c = jnp.concatenate([out_mat(c_re.astype(F32)), -out_mat(c_im.astype(F32))], axis=-2).astype(BF16)
    a_r = ab_re.reshape(nd, nb, gb * p)
    a_i = ab_im.reshape(nd, nb, gb * p)
    return w_b, w_c, a_r, a_i


def _s5_kernel(zc_ref, zl_ref, wb_ref, wc_ref, ar_ref, ai_ref, y_ref, sbuf, h_ref,
               *, n_ctx_chunks, chunk, nb, half, nbat):
    d = pl.program_id(1)
    c = pl.program_id(2)
    cw = 2 * half
    bw = wb_ref.shape[2]

    @pl.when(c == 0)
    def _():
        h_ref[...] = jnp.zeros_like(h_ref)

    nl = cw // LANES
    hl = half // LANES

    def project(src_ref):
        for e in range(nbat):
            for k in range(nb):
                xk = src_ref[e, :, k * bw:(k + 1) * bw]
                bu = jnp.dot(xk, wb_ref[0, k], preferred_element_type=F32)
                for l in range(nl):
                    sbuf[e, l, pl.ds(k, chunk, stride=nb), :] = bu[:, l * LANES:(l + 1) * LANES]

    def scan():
        a_r = [ar_ref[0, :, l * LANES:(l + 1) * LANES] for l in range(hl)]
        a_i = [ai_ref[0, :, l * LANES:(l + 1) * LANES] for l in range(hl)]

        def step(s, carry):
            t = s + d * (chunk - 1 - 2 * s)
            row = pl.multiple_of(t * nb, nb)
            out = []
            for e in range(nbat):
                for l in range(hl):
                    h_r, h_i = carry[2 * (e * hl + l)], carry[2 * (e * hl + l) + 1]
                    n_r = a_r[l] * h_r - a_i[l] * h_i + sbuf[e, l, pl.ds(row, nb), :]
                    n_i = a_r[l] * h_i + a_i[l] * h_r + sbuf[e, hl + l, pl.ds(row, nb), :]
                    sbuf[e, l, pl.ds(row, nb), :] = n_r
                    sbuf[e, hl + l, pl.ds(row, nb), :] = n_i
                    out += [n_r, n_i]
            return tuple(out)

        init = []
        for e in range(nbat):
            for l in range(hl):
                init += [h_ref[e, l], h_ref[e, hl + l]]
        fin = lax.fori_loop(0, chunk, step, tuple(init), unroll=2)
        for e in range(nbat):
            for l in range(hl):
                h_ref[e, l] = fin[2 * (e * hl + l)]
                h_ref[e, hl + l] = fin[2 * (e * hl + l) + 1]

    def readout():
        for e in range(nbat):
            for k in range(nb):
                hk = jnp.concatenate([sbuf[e, l, pl.ds(k, chunk, stride=nb), :] for l in range(nl)], axis=1)
                y_ref[0, e, :, k * bw:(k + 1) * bw] = jnp.dot(hk.astype(BF16), wc_ref[0, k],
                                                              preferred_element_type=F32)

    @pl.when(c < n_ctx_chunks)
    def _():
        project(zc_ref)
        scan()

    @pl.when(c >= n_ctx_chunks)
    def _():
        project(zl_ref)
        scan()
        readout()


def _s5_call(zc, zl, w_b, w_c, a_r, a_i, chunk):
    bsz, lc, width = zc.shape
    seq = zl.shape[1]
    nd, nb, bw, cw = w_b.shape
    half = cw // 2
    assert nb == 8 and lc % chunk == 0 and seq % chunk == 0 and width == nb * bw
    ncc, nlc = lc // chunk, seq // chunk
    nbat = 2 if bsz % 2 == 0 else 1

    def ctx_idx(b, d, c):
        cc = jnp.minimum(c, ncc - 1)
        return (b, cc + d * (ncc - 1 - 2 * cc), 0)

    def lat_chunk(d, c):
        lc_ = jnp.maximum(c - ncc, 0)
        return lc_ + d * (nlc - 1 - 2 * lc_)

    kern = functools.partial(_s5_kernel, n_ctx_chunks=ncc, chunk=chunk, nb=nb, half=half, nbat=nbat)
    return pl.pallas_call(
        kern,
        grid=(bsz // nbat, nd, ncc + nlc),
        in_specs=[pl.BlockSpec((nbat, chunk, width), ctx_idx),
                  pl.BlockSpec((nbat, chunk, width), lambda b, d, c: (b, lat_chunk(d, c), 0)),
                  pl.BlockSpec((1, nb, bw, cw), lambda b, d, c: (d, 0, 0, 0)),
                  pl.BlockSpec((1, nb, cw, bw), lambda b, d, c: (d, 0, 0, 0)),
                  pl.BlockSpec((1, nb, half), lambda b, d, c: (d, 0, 0)),
                  pl.BlockSpec((1, nb, half), lambda b, d, c: (d, 0, 0))],
        out_specs=pl.BlockSpec((1, nbat, chunk, width), lambda b, d, c: (d, b, lat_chunk(d, c), 0)),
        out_shape=jax.ShapeDtypeStruct((nd, bsz, seq, width), F32),
        scratch_shapes=[pltpu.VMEM((nbat, cw // LANES, chunk * nb, LANES), F32),
                        pltpu.VMEM((nbat, cw // LANES, nb, LANES), F32)],
        compiler_params=_params(("arbitrary", "arbitrary", "arbitrary")),
        name="s5_scan",
    )(zc, zl, w_b, w_c, a_r, a_i)


def _mix_kernel(x_ref, yf_ref, yb_ref, us_ref, up_ref, gt_ref, mod_ref, dsk_ref, wglu_ref, bglu_ref,
                pw_ref, psc_ref, pa_ref, pb_ref, wo_ref, nw2_ref, wq_ref,
                x1_ref, h2_ref, q_ref, *, d_model, tm):
    b = pl.program_id(0)
    gate1 = mod_ref[pl.ds(b, 1), 2 * d_model:3 * d_model]
    shift2 = mod_ref[pl.ds(b, 1), 3 * d_model:4 * d_model]
    scale2 = mod_ref[pl.ds(b, 1), 4 * d_model:5 * d_model]

    u_s = us_ref[0].astype(F32)
    y = yf_ref[0, 0] + yb_ref[0, 0] + dsk_ref[...] * u_s
    ge = _gelu_tanh(y)
    lin = jnp.dot(ge.astype(BF16), wglu_ref[...], preferred_element_type=F32) + bglu_ref[...]
    y_a = (ge * _sigmoid(lin)).astype(BF16)

    u_p = up_ref[0]
    ri = lax.broadcasted_iota(I32, (tm, tm), 0)
    ci = lax.broadcasted_iota(I32, (tm, tm), 1)
    same_row = (ri // GRID_W) == (ci // GRID_W)
    pos_r = ri % GRID_W
    pos_c = ci % GRID_W
    pos1 = lax.broadcasted_iota(I32, (tm, 1), 0) % GRID_W
    gw = pw_ref.shape[1]
    y_b_parts = []
    for gi, w in enumerate(POOL_WINDOWS):
        lo = jnp.maximum(pos_r - w // 2, 0)
        hi = jnp.minimum(pos_r + w // 2 - 1, GRID_W - 1)
        band = jnp.where(same_row & (pos_c >= lo) & (pos_c <= hi), 1.0, 0.0).astype(BF16)
        ug = u_p[:, gi * gw:(gi + 1) * gw]
        win = jnp.dot(band, ug, preferred_element_type=F32)
        cnt = (jnp.minimum(pos1 + w // 2 - 1, GRID_W - 1) - jnp.maximum(pos1 - w // 2, 0) + 1).astype(F32)
        p = win / cnt - ug.astype(F32)
        y_b_parts.append(jnp.dot(p.astype(BF16), pw_ref[gi], preferred_element_type=F32))
    y_b = (jnp.concatenate(y_b_parts, axis=-1) * psc_ref[...]).astype(BF16)

    g_a = _sigmoid(gt_ref[0, :, 0:d_model].astype(F32))
    g_b = _sigmoid(gt_ref[0, :, d_model:2 * d_model].astype(F32))
    m = (g_a * jnp.dot(y_a, pa_ref[...], preferred_element_type=F32)
         + g_b * jnp.dot(y_b, pb_ref[...], preferred_element_type=F32))
    x1 = x_ref[0] + gate1 * jnp.dot(m.astype(BF16), wo_ref[...], preferred_element_type=F32)
    x1_ref[0] = x1

    h2 = _modulated_norm(x1, nw2_ref[...], shift2, scale2).astype(BF16)
    h2_ref[0] = h2
    q_ref[0] = jnp.dot(h2, wq_ref[...], preferred_element_type=F32)


def _mix_call(x, y2, us, up, gates, mod, d_skip, w_glu, b_glu, pool_w, pool_scale,
              proj_a, proj_b, w_out, norm2_w, w_q, tm):
    bsz, seq, d_model = x.shape
    sw = us.shape[2]
    pwid = up.shape[2]
    assert seq % tm == 0 and tm % GRID_W == 0
    full = lambda a: pl.BlockSpec(a.shape, lambda b, i: (0,) * a.ndim)
    row = lambda a: a[None]
    tok = lambda wd: pl.BlockSpec((1, tm, wd), lambda b, i: (b, i, 0))
    args = (x, y2, y2, us, up, gates, mod, row(d_skip), w_glu.astype(BF16), row(b_glu),
            pool_w.astype(BF16), row(pool_scale), proj_a.astype(BF16), proj_b.astype(BF16),
            w_out.astype(BF16), row(norm2_w), w_q.astype(BF16))
    in_specs = [tok(d_model),
                pl.BlockSpec((1, 1, tm, sw), lambda b, i: (0, b, i, 0)),
                pl.BlockSpec((1, 1, tm, sw), lambda b, i: (1, b, i, 0)),
                tok(sw), tok(pwid), tok(2 * d_model)] + [full(a) for a in args[6:]]
    return pl.pallas_call(
        functools.partial(_mix_kernel, d_model=d_model, tm=tm),
        grid=(bsz, seq // tm),
        in_specs=in_specs,
        out_specs=[tok(d_model), tok(d_model), tok(d_model)],
        out_shape=[jax.ShapeDtypeStruct((bsz, seq, d_model), F32),
                   jax.ShapeDtypeStruct((bsz, seq, d_model), BF16),
                   jax.ShapeDtypeStruct((bsz, seq, d_model), F32)],
        compiler_params=_params(("arbitrary", "arbitrary")),
        name="mix",
    )(*args)


def _oddeven_merge_sort_pairs(n):
    pairs = []

    def merge(lo, hi, r):
        step = r * 2
        if step < hi - lo:
            merge(lo, hi, step)
            merge(lo + r, hi, step)
            pairs.extend((i, i + r) for i in range(lo + r, hi - r, step))
        else:
            pairs.append((lo, lo + r))

    def sort(lo, hi):
        if hi > lo:
            mid = lo + (hi - lo) // 2
            sort(lo, mid)
            sort(mid + 1, hi)
            merge(lo, hi, 1)

    sort(0, n - 1)
    return pairs


_SORT16_PAIRS = _oddeven_merge_sort_pairs(PEER_KEYS // 8)


def _route_kernel(q_ref, k1_ref, k2_ref, o_ref,
                  v1s, x1s, v2s, x2s, es, gws, tes, tgs, *, tq, rpb):
    neg = -jnp.inf
    nk = PEER_KEYS
    ns = PEER_HEADS * PEER_TOPK
    sub8 = lax.broadcasted_iota(I32, (8, tq), 0)
    lt2, lt4, lt5, lt6 = sub8 < 2, sub8 < 4, sub8 < 5, sub8 < 6
    s8 = sub8.astype(F32)
    cio = jnp.concatenate([s8, s8 + 8.0, s8 + 16.0,
                           jnp.where(lt5, s8 + 32.0, s8 + 59.0),
                           jnp.where(lt4, s8 + 48.0, jnp.where(lt6, s8 + 76.0, s8 + 90.0)),
                           jnp.where(lt2, s8 + 112.0, s8 * 16.0 + 96.0),
                           s8 * 16.0 + 224.0], axis=0)
    big = 1024.0
    sub_s = lax.broadcasted_iota(I32, (nk, ns), 0).astype(F32)
    hd = q_ref.shape[1] // PEER_HEADS
    half = hd // 2
    tokens_per_head = tq // PEER_HEADS

    @pl.when(pl.program_id(0) == 0)
    def _():
        tes[...] = jnp.zeros_like(tes)
        tgs[...] = jnp.zeros_like(tgs)

    def top16(s, vals_ref, idx_ref):
        val = [s[8 * v:8 * v + 8, :] for v in range(nk // 8)]
        idx = [s8 + float(8 * v) for v in range(nk // 8)]
        for (a, b) in _SORT16_PAIRS:
            va, vb, ia, ib = val[a], val[b], idx[a], idx[b]
            first = (va > vb) | ((va == vb) & (ia < ib))
            val[a], val[b] = jnp.maximum(va, vb), jnp.minimum(va, vb)
            idx[a], idx[b] = jnp.where(first, ia, ib), jnp.where(first, ib, ia)
        for k in range(PEER_TOPK):
            m = jnp.max(val[0], axis=0, keepdims=True)
            i = jnp.min(jnp.where(val[0] == m, idx[0], big), axis=0, keepdims=True)
            vals_ref[k:k + 1, :] = m
            idx_ref[k:k + 1, :] = i
            win = idx[0] == i
            for d in range(PEER_TOPK - 1 - k):
                val[d] = jnp.where(win, val[d + 1], val[d])
                idx[d] = jnp.where(win, idx[d + 1], idx[d])

    def route_head(h):
        q1 = q_ref[:, h * hd:h * hd + half]
        q2 = q_ref[:, h * hd + half:(h + 1) * hd]
        nt = (((1,), (1,)), ((), ()))
        s1 = lax.dot_general(k1_ref[...], q1, nt, precision=lax.Precision.HIGHEST,
                             preferred_element_type=F32)
        s2 = lax.dot_general(k2_ref[...], q2, nt, precision=lax.Precision.HIGHEST,
                             preferred_element_type=F32)
        top16(s1, v1s, x1s)
        top16(s2, v2s, x2s)

        v2a, v2b = v2s[0:8, :], v2s[8:16, :]
        x2a, x2b = x2s[0:8, :], x2s[8:16, :]
        r1v = pltpu.roll(v1s[8:16, :], 2, 0)
        r1x = pltpu.roll(x1s[8:16, :], 2, 0)
        v1r = lambda p: v1s[p:p + 1, :]
        x1r = lambda p: x1s[p:p + 1, :]
        pieces = [
            (v1r(0) + v2a, x1r(0), x2a),
            (v1r(0) + v2b, x1r(0), x2b),
            (v1r(1) + v2a, x1r(1), x2a),
            (jnp.where(lt5, v1r(2), v1r(4)) + jnp.where(lt5, v2a, pltpu.roll(v2a, 5, 0)),
             jnp.where(lt5, x1r(2), x1r(4)), jnp.where(lt5, x2a, pltpu.roll(x2a, 5, 0))),
            (jnp.where(lt4, v1r(3), jnp.where(lt6, v1r(5), v1r(6)))
             + jnp.where(lt4, v2a, jnp.where(lt6, pltpu.roll(v2a, 4, 0), pltpu.roll(v2a, 6, 0))),
             jnp.where(lt4, x1r(3), jnp.where(lt6, x1r(5), x1r(6))),
             jnp.where(lt4, x2a, jnp.where(lt6, pltpu.roll(x2a, 4, 0), pltpu.roll(x2a, 6, 0)))),
            (jnp.where(lt2, v1r(7), r1v) + jnp.where(lt2, v2a, v2s[0:1, :]),
             jnp.where(lt2, x1r(7), r1x), jnp.where(lt2, x2a, x2s[0:1, :])),
            (jnp.where(lt2, r1v + v2s[0:1, :], neg), r1x, jnp.broadcast_to(x2s[0:1, :], (8, tq))),
        ]
        cand = jnp.concatenate([pc[0] for pc in pieces], axis=0)
        ceid = jnp.concatenate([jnp.broadcast_to(pc[1], (8, tq)) * float(nk) + pc[2] for pc in pieces], axis=0)

        tops = []
        for k in range(PEER_TOPK):
            m = jnp.max(cand, axis=0, keepdims=True)
            i = jnp.min(jnp.where(cand == m, cio, big), axis=0, keepdims=True)
            hit = cio == i
            r = h * PEER_TOPK + k
            es[r:r + 1, :] = jnp.sum(jnp.where(hit, ceid, 0.0), axis=0, keepdims=True)
            tops.append(m)
            cand = jnp.where(hit, neg, cand)
        ex = [jnp.exp(t - tops[0]) for t in tops]
        tot = ex[0]
        for e in ex[1:]:
            tot = tot + e
        for k in range(PEER_TOPK):
            r = h * PEER_TOPK + k
            gws[r:r + 1, :] = ex[k] / tot

    def expand_tokens(t0):
        eid_8 = tes[t0:t0 + 8, :]
        i1_8 = jnp.floor(eid_8 * (1.0 / nk))
        i2_8 = eid_8 - i1_8 * float(nk)
        gw_8 = tgs[t0:t0 + 8, :]
        gh_8 = gw_8.astype(BF16).astype(F32)
        gl_8 = gw_8 - gh_8
        for u in range(8):
            hit1 = sub_s == i1_8[u:u + 1, :]
            a2 = jnp.concatenate([jnp.where(hit1, gh_8[u:u + 1, :], 0.0),
                                  jnp.where(hit1, gl_8[u:u + 1, :], 0.0)], axis=1).astype(BF16)
            b1 = jnp.where(sub_s == i2_8[u:u + 1, :], 1.0, 0.0)
            b2 = jnp.concatenate([b1, b1], axis=1).astype(BF16)
            gt = lax.dot_general(a2, b2, (((1,), (1,)), ((), ())), preferred_element_type=F32)
            row0 = (t0 + u) * rpb
            for jb in range(nk // rpb):
                o_ref[0, jb, row0:row0 + rpb, :] = gt[jb * rpb:(jb + 1) * rpb, :]

    for h in range(PEER_HEADS):
        route_head(h)
        for t0 in range(h * tokens_per_head, (h + 1) * tokens_per_head, 8):
            expand_tokens(t0)

    tes[...] = es[...].T
    tgs[...] = gws[...].T


def _route_call(q2d, k1, k2, tq, tm, rpb):
    n_tok, d_q = q2d.shape
    nk = PEER_KEYS
    ns = PEER_HEADS * PEER_TOPK
    assert n_tok % tm == 0 and tm % tq == 0 and ns == nk and tq == LANES and nk % rpb == 0
    n_tiles = n_tok // tq
    per = tm // tq

    def out_idx(i):
        p = jnp.maximum(i - 1, 0)
        return (p // per, 0, p % per, 0)

    return pl.pallas_call(
        functools.partial(_route_kernel, tq=tq, rpb=rpb),
        grid=(n_tiles + 1,),
        in_specs=[pl.BlockSpec((tq, d_q), lambda i: (jnp.minimum(i, n_tiles - 1), 0)),
                  pl.BlockSpec(k1.shape, lambda i: (0, 0)),
                  pl.BlockSpec(k2.shape, lambda i: (0, 0))],
        out_specs=pl.BlockSpec((1, nk // rpb, tq * rpb, nk), out_idx),
        out_shape=jax.ShapeDtypeStruct((n_tok // tm, nk // rpb, tm * rpb, nk), F32),
        scratch_shapes=[pltpu.VMEM((PEER_TOPK, tq), F32), pltpu.VMEM((PEER_TOPK, tq), F32),
                        pltpu.VMEM((PEER_TOPK, tq), F32), pltpu.VMEM((PEER_TOPK, tq), F32),
                        pltpu.VMEM((ns, tq), F32), pltpu.VMEM((ns, tq), F32),
                        pltpu.VMEM((tq, ns), F32), pltpu.VMEM((tq, ns), F32)],
        compiler_params=_params(("arbitrary",)),
        name="peer_route",
    )(q2d, k1, k2)


def _peer_kernel(h_ref, u_ref, v_ref, g_ref, x1_ref, mod_ref, fw_ref, o_ref,
                 *, d_model, rpb, tokens_per_batch, tm):
    i = pl.program_id(0)
    j = pl.program_id(1)

    @pl.when(j == 0)
    def _():
        o_ref[...] = jnp.zeros_like(o_ref)

    z = lax.dot_general(h_ref[...], u_ref[...], (((1,), (1,)), ((), ())), preferred_element_type=F32)
    gmat = jnp.concatenate([g_ref[0, 0, pl.ds(r, tm, stride=rpb), :] for r in range(rpb)], axis=1)
    w = (_gelu_tanh(z) * gmat).astype(BF16)
    o_ref[...] += jnp.dot(w, v_ref[...], preferred_element_type=F32)

    @pl.when(j == pl.num_programs(1) - 1)
    def _():
        b = (i * tm) // tokens_per_batch
        gate2 = mod_ref[pl.ds(b, 1), 5 * d_model:6 * d_model]
        x2 = x1_ref[...] + gate2 * o_ref[...]
        ms = jnp.mean(x2 * x2, axis=-1, keepdims=True)
        o_ref[...] = x2 * lax.rsqrt(ms + EPS) * fw_ref[...]


def _peer_call(h2, u_bf, v_bf, gmat, x1, mod, final_w, tokens_per_batch, tm, rpb):
    n_tok, d_model = h2.shape
    n_exp = v_bf.shape[0]
    nk = PEER_KEYS
    eb = rpb * nk
    assert n_tok % tm == 0 and n_exp % eb == 0 and tokens_per_batch % tm == 0
    assert gmat.shape == (n_tok // tm, n_exp // eb, tm * rpb, nk)
    kern = functools.partial(_peer_kernel, d_model=d_model, rpb=rpb,
                             tokens_per_batch=tokens_per_batch, tm=tm)
    return pl.pallas_call(
        kern,
        grid=(n_tok // tm, n_exp // eb),
        in_specs=[pl.BlockSpec((tm, d_model), lambda i, j: (i, 0)),
                  pl.BlockSpec((eb, d_model), lambda i, j: (j, 0)),
                  pl.BlockSpec((eb, d_model), lambda i, j: (j, 0)),
                  pl.BlockSpec((1, 1, tm * rpb, nk), lambda i, j: (i, j, 0, 0)),
                  pl.BlockSpec((tm, d_model), lambda i, j: (i, 0), pipeline_mode=pl.Buffered(1)),
                  pl.BlockSpec(mod.shape, lambda i, j: (0, 0)),
                  pl.BlockSpec((1, d_model), lambda i, j: (0, 0))],
        out_specs=pl.BlockSpec((tm, d_model), lambda i, j: (i, 0)),
        out_shape=jax.ShapeDtypeStruct((n_tok, d_model), F32),
        compiler_params=_params(("arbitrary", "arbitrary")),
        name="peer_experts",
    )(h2, u_bf, v_bf, gmat, x1, mod, final_w[None])


def kernel(x, c, ctx, c_ctx, w_mod, b_mod, norm1_w, norm2_w, w_in, s5_a_re, s5_a_im, s5_log_dt,
           s5_b_re, s5_b_im, s5_c_re, s5_c_im, s5_d, w_glu, b_glu, pool_w, pool_scale, proj_a,
           proj_b, w_out, peer_wq, peer_k1, peer_k2, peer_u, peer_v, final_w):
    bsz, seq, d_model = x.shape
    depth = w_mod.shape[0]
    assert depth == 1, "only the single-layer configuration is implemented"
    l = 0
    s5_width = s5_d.shape[1]
    pool_width = pool_scale.shape[1]

    mod = _mod_call(c, c_ctx, w_mod[l], b_mod[l])

    w_in_bf = w_in[l].astype(BF16)
    tm = min(256, seq)
    us, up, gates = _inproj_call(x, mod, norm1_w[l], w_in_bf,
                                 (s5_width, pool_width, 2 * d_model), None, tm, "inproj_lat")
    lc = ctx.shape[1]
    (zc,) = _inproj_call(ctx, mod, norm1_w[l], w_in_bf[:, :s5_width], (s5_width,), bsz,
                         min(256, lc), "inproj_ctx")

    ab_re, ab_im, bb_re, bb_im = _disc_call(s5_a_re[l], s5_a_im[l], s5_log_dt[l], s5_b_re[l], s5_b_im[l])
    w_b, w_c, a_r, a_i = _s5_weights(ab_re, ab_im, bb_re, bb_im, s5_c_re[l], s5_c_im[l])
    chunk = min(256, lc, seq)
    y2 = _s5_call(zc, us, w_b, w_c, a_r, a_i, chunk)

    x1, h2, q = _mix_call(x, y2, us, up, gates, mod, s5_d[l], w_glu[l], b_glu[l], pool_w[l],
                          pool_scale[l], proj_a[l], proj_b[l], w_out[l], norm2_w[l], peer_wq[l], tm)

    n_tok = bsz * seq
    tm_peer, rpb = min(1024, seq), 4
    gmat = _route_call(q.reshape(n_tok, d_model), peer_k1[l], peer_k2[l], LANES, tm_peer, rpb)
    out = _peer_call(h2.reshape(n_tok, d_model), peer_u[l].astype(BF16), peer_v[l].astype(BF16),
                     gmat, x1.reshape(n_tok, d_model), mod, final_w, seq, tm_peer, rpb)
    return out.reshape(bsz, seq, d_model)
```

```python
import functools
import math

import jax
import jax.numpy as jnp
from jax import lax
from jax.experimental import pallas as pl
from jax.experimental.pallas import tpu as pltpu

F32 = jnp.float32
BF16 = jnp.bfloat16
I32 = jnp.int32

LANES = 128
EPS = 1e-6
GRID_W = 64
S5_GROUP = 16
S5_STATE = 64
S5_GROUPS_PER_BLOCK = 8
POOL_WINDOWS = (2, 4, 8, 16)
PEER_HEADS = 8
PEER_KEYS = 128
PEER_TOPK = 16
N_MOD = 6

V7X_VMEM_BYTES = 64 * 1024 * 1024
VMEM_LIMIT = 56 * 1024 * 1024


def _params(sem, vmem=VMEM_LIMIT):
    return pltpu.CompilerParams(dimension_semantics=sem, vmem_limit_bytes=vmem)


def _gelu_tanh(x):
    return 0.5 * x * (1.0 + jnp.tanh(math.sqrt(2.0 / math.pi) * (x + 0.044715 * (x * x * x))))


def _sigmoid(x):
    return 1.0 / (1.0 + jnp.exp(-x))


def _mod_kernel(cb_ref, w_ref, b_ref, o_ref, a_scr, *, n_rows):
    d_model, tn = w_ref.shape
    reps = tn // LANES

    @pl.when(pl.program_id(0) == 0)
    def _():
        cb = cb_ref[...]
        a_scr[...] = cb * _sigmoid(cb)

    def body(kb, accs):
        k0 = pl.multiple_of(kb * 8, 8)
        wt = w_ref[pl.ds(k0, 8), :]
        return tuple(acc + jnp.concatenate([a_scr[r, pl.ds(k0, 8), :]] * reps, axis=1) * wt
                     for r, acc in enumerate(accs))

    accs = lax.fori_loop(0, d_model // 8, body,
                         tuple(jnp.zeros((8, tn), F32) for _ in range(n_rows)), unroll=2)
    for r in range(8):
        if r < n_rows:
            o_ref[r:r + 1, :] = jnp.sum(accs[r], axis=0, keepdims=True) + b_ref[...]
        else:
            o_ref[r:r + 1, :] = jnp.zeros((1, tn), F32)


def _mod_call(c, c_ctx, w_mod, b_mod):
    bsz, d_model = c.shape
    n_out = w_mod.shape[1]
    n_rows = bsz + 1
    assert n_rows <= 8
    cb = jnp.broadcast_to(jnp.concatenate([c, c_ctx[None]], axis=0)[:, :, None], (n_rows, d_model, LANES))
    tn = 512
    return pl.pallas_call(
        functools.partial(_mod_kernel, n_rows=n_rows),
        grid=(n_out // tn,),
        in_specs=[pl.BlockSpec((n_rows, d_model, LANES), lambda j: (0, 0, 0)),
                  pl.BlockSpec((d_model, tn), lambda j: (0, j)),
                  pl.BlockSpec((1, tn), lambda j: (0, j))],
        out_specs=pl.BlockSpec((8, tn), lambda j: (0, j)),
        out_shape=jax.ShapeDtypeStruct((8, n_out), F32),
        scratch_shapes=[pltpu.VMEM((n_rows, d_model, LANES), F32)],
        compiler_params=_params(("arbitrary",)),
        name="mod",
    )(cb, w_mod, b_mod[None])


def _modulated_norm(x, nw, shift, scale):
    ms = jnp.mean(x * x, axis=-1, keepdims=True)
    y = x * lax.rsqrt(ms + EPS) * nw
    return y * (1.0 + scale) + shift


def _inproj_kernel(x_ref, mod_ref, nw_ref, w_ref, *o_refs, d_model, widths, ctx_row):
    row = ctx_row if ctx_row is not None else pl.program_id(0)
    shift = mod_ref[pl.ds(row, 1), 0:d_model]
    scale = mod_ref[pl.ds(row, 1), d_model:2 * d_model]
    h = _modulated_norm(x_ref[0], nw_ref[...], shift, scale).astype(BF16)
    c0 = 0
    for o_ref, wd in zip(o_refs, widths):
        o_ref[0] = jnp.dot(h, w_ref[:, c0:c0 + wd], preferred_element_type=F32).astype(o_ref.dtype)
        c0 += wd


def _inproj_call(x, mod, norm_w, w_bf, widths, ctx_row, tm, name):
    bsz, seq, d_model = x.shape
    n_in = w_bf.shape[1]
    assert sum(widths) == n_in and seq % tm == 0
    kern = functools.partial(_inproj_kernel, d_model=d_model, widths=widths, ctx_row=ctx_row)
    return pl.pallas_call(
        kern,
        grid=(bsz, seq // tm),
        in_specs=[pl.BlockSpec((1, tm, d_model), lambda b, i: (b, i, 0)),
                  pl.BlockSpec(mod.shape, lambda b, i: (0, 0)),
                  pl.BlockSpec((1, d_model), lambda b, i: (0, 0)),
                  pl.BlockSpec((d_model, n_in), lambda b, i: (0, 0))],
        out_specs=[pl.BlockSpec((1, tm, wd), lambda b, i: (b, i, 0)) for wd in widths],
        out_shape=[jax.ShapeDtypeStruct((bsz, seq, wd), BF16) for wd in widths],
        compiler_params=_params(("arbitrary", "arbitrary")),
        name=name,
    )(x, mod, norm_w[None], w_bf)


def _disc_kernel(are_ref, aim_ref, ldt_ref, bre_ref, bim_ref,
                 abr_ref, abi_ref, bbr_ref, bbi_ref, bar_ref, bai_ref):
    a_re = are_ref[...]
    a_im = aim_ref[...]
    dt = jnp.exp(ldt_ref[...])
    mag = jnp.exp(dt * a_re)
    ab_re = mag * jnp.cos(dt * a_im)
    ab_im = mag * jnp.sin(dt * a_im)
    den = a_re * a_re + a_im * a_im
    nr = ab_re - 1.0
    ni = ab_im
    f_re = (nr * a_re + ni * a_im) / den
    f_im = (ni * a_re - nr * a_im) / den
    b_re = bre_ref[...]
    b_im = bim_ref[...]
    bb_re = f_re * b_re - f_im * b_im
    bb_im = f_re * b_im + f_im * b_re
    abr_ref[...] = ab_re
    abi_ref[...] = ab_im
    bbr_ref[...] = bb_re
    bbi_ref[...] = bb_im
    bar_ref[...] = ab_re * bb_re - ab_im * bb_im
    bai_ref[...] = ab_re * bb_im + ab_im * bb_re


def _disc_call(a_re, a_im, log_dt, b_re, b_im):
    nd, g, p, h = b_re.shape
    rows, cols = nd * g, p * h
    rep = lambda a: jnp.broadcast_to(a[..., None], (nd, g, p, h)).reshape(rows, cols)
    args = (rep(a_re), rep(a_im),
            jnp.broadcast_to(log_dt[..., None, None], (nd, g, p, h)).reshape(rows, cols),
            b_re.reshape(rows, cols), b_im.reshape(rows, cols))
    outs = pl.pallas_call(
        _disc_kernel,
        out_shape=[jax.ShapeDtypeStruct((rows, cols), F32)] * 6,
        name="s5_disc",
    )(*args)
    ab_re, ab_im, bb_re, bb_im, ba_re, ba_im = (o.reshape(nd, g, p, h) for o in outs)
    return ab_re[..., 0], ab_im[..., 0], bb_re, bb_im, ba_re, ba_im


def _s5_weights(ab_re, ab_im, bb_re, bb_im, ba_re, ba_im, c_re, c_im):
    nd, g, p, h = bb_re.shape
    gb = S5_GROUPS_PER_BLOCK
    nb = g // gb
    eye = jnp.eye(gb, dtype=bool)

    def in_mat(bb):
        t = bb.reshape(nd, nb, gb, p, h).transpose(0, 1, 2, 4, 3)
        t = jnp.where(eye[None, None, :, None, :, None], t[:, :, :, :, None, :], 0.0)
        return t.reshape(nd, nb, gb * h, gb * p)

    def out_mat(cc):
        t = cc.reshape(nd, nb, gb, h, p).transpose(0, 1, 2, 4, 3)
        t = jnp.where(eye[None, None, :, None, :, None], t[:, :, :, :, None, :], 0.0)
        return t.reshape(nd, nb, gb * p, gb * h)

    w_own = jnp.concatenate([in_mat(bb_re), in_mat(bb_im)], axis=-1)
    w_par = jnp.concatenate([in_mat(ba_re), in_mat(ba_im)], axis=-1)
    w_b = jnp.concatenate([w_own, w_par], axis=-2).astype(BF16)
    w_c = jnp.concatenate([out_mat(c_re.astype(F32)), -out_mat(c_im.astype(F32))], axis=-2).astype(BF16)
    a_r = ab_re.reshape(nd, nb, gb * p)
    a_i = ab_im.reshape(nd, nb, gb * p)
    return w_b, w_c, a_r, a_i


S5_ROWS = 4


def _s5_kernel(zc_ref, zl_ref, wb_ref, wc_ref, ar_ref, ai_ref, y_ref, sbuf, h_ref,
               *, n_ctx_chunks, chunk, nb, half, nbat):
    d = pl.program_id(1)
    c = pl.program_id(2)
    cw = 2 * half
    bw = wb_ref.shape[2] // 2
    nl = cw // LANES
    hl = half // LANES
    nhb = nb // S5_ROWS
    npair = chunk // 2

    @pl.when(c == 0)
    def _():
        h_ref[...] = jnp.zeros_like(h_ref)

    sub8 = lax.broadcasted_iota(I32, (8, LANES), 0)
    second = (sub8 >= S5_ROWS) == (d == 0)
    row_par = lax.broadcasted_iota(I32, (chunk, 1), 0) % 2
    is_second = row_par == (1 - d)

    def project(src_ref):
        for e in range(nbat):
            x = src_ref[e]
            xf = x.astype(F32)
            partner = jnp.where(d == 0, pltpu.roll(xf, 1, 0), pltpu.roll(xf, chunk - 1, 0))
            xp = jnp.where(is_second, partner, 0.0).astype(BF16)
            for k in range(nb):
                hb, kk = divmod(k, S5_ROWS)
                lhs = jnp.concatenate([x[:, k * bw:(k + 1) * bw], xp[:, k * bw:(k + 1) * bw]], axis=1)
                bu = jnp.dot(lhs, wb_ref[0, k], preferred_element_type=F32)
                for l in range(nl):
                    sbuf[e, hb * nl + l, pl.ds(kk, chunk, stride=S5_ROWS), :] = bu[:, l * LANES:(l + 1) * LANES]

    def scan():
        p_r, p_i = [], []
        for hb in range(nhb):
            for l in range(hl):
                a_r = ar_ref[0, hb * S5_ROWS:(hb + 1) * S5_ROWS, l * LANES:(l + 1) * LANES]
                a_i = ai_ref[0, hb * S5_ROWS:(hb + 1) * S5_ROWS, l * LANES:(l + 1) * LANES]
                a_r = jnp.concatenate([a_r, a_r], axis=0)
                a_i = jnp.concatenate([a_i, a_i], axis=0)
                p_r.append(jnp.where(second, a_r * a_r - a_i * a_i, a_r))
                p_i.append(jnp.where(second, 2.0 * a_r * a_i, a_i))
        nch = nhb * hl

        for e in range(nbat):
            def step(s, carry):
                m = s + d * (npair - 1 - 2 * s)
                row = pl.multiple_of(m * 8, 8)
                out = []
                for hb in range(nhb):
                    for l in range(hl):
                        ch = hb * hl + l
                        c_r, c_i = carry[2 * ch], carry[2 * ch + 1]
                        pr, pi = hb * nl + l, hb * nl + hl + l
                        n_r = p_r[ch] * c_r - p_i[ch] * c_i + sbuf[e, pr, pl.ds(row, 8), :]
                        n_i = p_r[ch] * c_i + p_i[ch] * c_r + sbuf[e, pi, pl.ds(row, 8), :]
                        sbuf[e, pr, pl.ds(row, 8), :] = n_r
                        sbuf[e, pi, pl.ds(row, 8), :] = n_i
                        out += [jnp.where(second, n_r, pltpu.roll(n_r, S5_ROWS, 0)),
                                jnp.where(second, n_i, pltpu.roll(n_i, S5_ROWS, 0))]
                return tuple(out)

            init = []
            for ch in range(nch):
                init += [h_ref[e, 2 * ch], h_ref[e, 2 * ch + 1]]
            fin = lax.fori_loop(0, npair, step, tuple(init), unroll=2)
            for ch in range(nch):
                h_ref[e, 2 * ch] = fin[2 * ch]
                h_ref[e, 2 * ch + 1] = fin[2 * ch + 1]

    def readout():
        for e in range(nbat):
            for k in range(nb):
                hb, kk = divmod(k, S5_ROWS)
                hk = jnp.concatenate([sbuf[e, hb * nl + l, pl.ds(kk, chunk, stride=S5_ROWS), :]
                                      for l in range(nl)], axis=1)
                y_ref[0, e, :, k * bw:(k + 1) * bw] = jnp.dot(hk.astype(BF16), wc_ref[0, k],
                                                              preferred_element_type=F32)

    @pl.when(c < n_ctx_chunks)
    def _():
        project(zc_ref)
        scan()

    @pl.when(c >= n_ctx_chunks)
    def _():
        project(zl_ref)
        scan()
        readout()


def _s5_call(zc, zl, w_b, w_c, a_r, a_i, chunk):
    bsz, lc, width = zc.shape
    seq = zl.shape[1]
    nd, nb, bw2, cw = w_b.shape
    bw = bw2 // 2
    half = cw // 2
    assert nb % S5_ROWS == 0 and lc % chunk == 0 and seq % chunk == 0 and width == nb * bw and chunk % 8 == 0
    ncc, nlc = lc // chunk, seq // chunk
    nbat = 2 if bsz % 2 == 0 else 1
    nplanes = (nb // S5_ROWS) * (cw // LANES)

    def ctx_idx(b, d, c):
        cc = jnp.minimum(c, ncc - 1)
        return (b, cc + d * (ncc - 1 - 2 * cc), 0)

    def lat_chunk(d, c):
        lc_ = jnp.maximum(c - ncc, 0)
        return lc_ + d * (nlc - 1 - 2 * lc_)

    kern = functools.partial(_s5_kernel, n_ctx_chunks=ncc, chunk=chunk, nb=nb, half=half, nbat=nbat)
    return pl.pallas_call(
        kern,
        grid=(bsz // nbat, nd, ncc + nlc),
        in_specs=[pl.BlockSpec((nbat, chunk, width), ctx_idx),
                  pl.BlockSpec((nbat, chunk, width), lambda b, d, c: (b, lat_chunk(d, c), 0)),
                  pl.BlockSpec((1, nb, bw2, cw), lambda b, d, c: (d, 0, 0, 0)),
                  pl.BlockSpec((1, nb, cw, bw), lambda b, d, c: (d, 0, 0, 0)),
                  pl.BlockSpec((1, nb, half), lambda b, d, c: (d, 0, 0)),
                  pl.BlockSpec((1, nb, half), lambda b, d, c: (d, 0, 0))],
        out_specs=pl.BlockSpec((1, nbat, chunk, width), lambda b, d, c: (d, b, lat_chunk(d, c), 0)),
        out_shape=jax.ShapeDtypeStruct((nd, bsz, seq, width), F32),
        scratch_shapes=[pltpu.VMEM((nbat, nplanes, chunk * S5_ROWS, LANES), F32),
                        pltpu.VMEM((nbat, nplanes, 8, LANES), F32)],
        compiler_params=_params(("arbitrary", "arbitrary", "arbitrary")),
        name="s5_scan",
    )(zc, zl, w_b, w_c, a_r, a_i)


def _mix_kernel(x_ref, yf_ref, yb_ref, us_ref, up_ref, gt_ref, mod_ref, dsk_ref, wglu_ref, bglu_ref,
                pw_ref, psc_ref, pa_ref, pb_ref, wo_ref, nw2_ref, wq_ref,
                x1_ref, h2_ref, q_ref, *, d_model, tm):
    b = pl.program_id(0)
    gate1 = mod_ref[pl.ds(b, 1), 2 * d_model:3 * d_model]
    shift2 = mod_ref[pl.ds(b, 1), 3 * d_model:4 * d_model]
    scale2 = mod_ref[pl.ds(b, 1), 4 * d_model:5 * d_model]

    u_s = us_ref[0].astype(F32)
    y = yf_ref[0, 0] + yb_ref[0, 0] + dsk_ref[...] * u_s
    ge = _gelu_tanh(y)
    lin = jnp.dot(ge.astype(BF16), wglu_ref[...], preferred_element_type=F32) + bglu_ref[...]
    y_a = (ge * _sigmoid(lin)).astype(BF16)

    u_p = up_ref[0]
    ri = lax.broadcasted_iota(I32, (tm, tm), 0)
    ci = lax.broadcasted_iota(I32, (tm, tm), 1)
    same_row = (ri // GRID_W) == (ci // GRID_W)
    pos_r = ri % GRID_W
    pos_c = ci % GRID_W
    pos1 = lax.broadcasted_iota(I32, (tm, 1), 0) % GRID_W
    gw = pw_ref.shape[1]
    y_b_parts = []
    for gi, w in enumerate(POOL_WINDOWS):
        lo = jnp.maximum(pos_r - w // 2, 0)
        hi = jnp.minimum(pos_r + w // 2 - 1, GRID_W - 1)
        band = jnp.where(same_row & (pos_c >= lo) & (pos_c <= hi), 1.0, 0.0).astype(BF16)
        ug = u_p[:, gi * gw:(gi + 1) * gw]
        win = jnp.dot(band, ug, preferred_element_type=F32)
        cnt = (jnp.minimum(pos1 + w // 2 - 1, GRID_W - 1) - jnp.maximum(pos1 - w // 2, 0) + 1).astype(F32)
        p = win / cnt - ug.astype(F32)
        y_b_parts.append(jnp.dot(p.astype(BF16), pw_ref[gi], preferred_element_type=F32))
    y_b = (jnp.concatenate(y_b_parts, axis=-1) * psc_ref[...]).astype(BF16)

    g_a = _sigmoid(gt_ref[0, :, 0:d_model].astype(F32))
    g_b = _sigmoid(gt_ref[0, :, d_model:2 * d_model].astype(F32))
    m = (g_a * jnp.dot(y_a, pa_ref[...], preferred_element_type=F32)
         + g_b * jnp.dot(y_b, pb_ref[...], preferred_element_type=F32))
    x1 = x_ref[0] + gate1 * jnp.dot(m.astype(BF16), wo_ref[...], preferred_element_type=F32)
    x1_ref[0] = x1

    h2 = _modulated_norm(x1, nw2_ref[...], shift2, scale2).astype(BF16)
    h2_ref[0] = h2
    q_ref[0] = jnp.dot(h2, wq_ref[...], preferred_element_type=F32)


def _mix_call(x, y2, us, up, gates, mod, d_skip, w_glu, b_glu, pool_w, pool_scale,
              proj_a, proj_b, w_out, norm2_w, w_q, tm):
    bsz, seq, d_model = x.shape
    sw = us.shape[2]
    pwid = up.shape[2]
    assert seq % tm == 0 and tm % GRID_W == 0
    full = lambda a: pl.BlockSpec(a.shape, lambda b, i: (0,) * a.ndim)
    row = lambda a: a[None]
    tok = lambda wd: pl.BlockSpec((1, tm, wd), lambda b, i: (b, i, 0))
    args = (x, y2, y2, us, up, gates, mod, row(d_skip), w_glu.astype(BF16), row(b_glu),
            pool_w.astype(BF16), row(pool_scale), proj_a.astype(BF16), proj_b.astype(BF16),
            w_out.astype(BF16), row(norm2_w), w_q.astype(BF16))
    in_specs = [tok(d_model),
                pl.BlockSpec((1, 1, tm, sw), lambda b, i: (0, b, i, 0)),
                pl.BlockSpec((1, 1, tm, sw), lambda b, i: (1, b, i, 0)),
                tok(sw), tok(pwid), tok(2 * d_model)] + [full(a) for a in args[6:]]
    return pl.pallas_call(
        functools.partial(_mix_kernel, d_model=d_model, tm=tm),
        grid=(bsz, seq // tm),
        in_specs=in_specs,
        out_specs=[tok(d_model), tok(d_model), tok(d_model)],
        out_shape=[jax.ShapeDtypeStruct((bsz, seq, d_model), F32),
                   jax.ShapeDtypeStruct((bsz, seq, d_model), BF16),
                   jax.ShapeDtypeStruct((bsz, seq, d_model), F32)],
        compiler_params=_params(("arbitrary", "arbitrary")),
        name="mix",
    )(*args)


def _oddeven_merge_sort_pairs(n):
    pairs = []

    def merge(lo, hi, r):
        step = r * 2
        if step < hi - lo:
            merge(lo, hi, step)
            merge(lo + r, hi, step)
            pairs.extend((i, i + r) for i in range(lo + r, hi - r, step))
        else:
            pairs.append((lo, lo + r))

    def sort(lo, hi):
        if hi > lo:
            mid = lo + (hi - lo) // 2
            sort(lo, mid)
            sort(mid + 1, hi)
            merge(lo, hi, 1)

    sort(0, n - 1)
    return pairs


_SORT16_PAIRS = _oddeven_merge_sort_pairs(PEER_KEYS // 8)


def _route_kernel(q_ref, k1_ref, k2_ref, o_ref,
                  v1s, x1s, v2s, x2s, es, gws, tes, tgs, *, tq, rpb):
    neg = -jnp.inf
    nk = PEER_KEYS
    ns = PEER_HEADS * PEER_TOPK
    sub8 = lax.broadcasted_iota(I32, (8, tq), 0)
    lt2, lt4, lt5, lt6 = sub8 < 2, sub8 < 4, sub8 < 5, sub8 < 6
    s8 = sub8.astype(F32)
    cio = jnp.concatenate([s8, s8 + 8.0, s8 + 16.0,
                           jnp.where(lt5, s8 + 32.0, s8 + 59.0),
                           jnp.where(lt4, s8 + 48.0, jnp.where(lt6, s8 + 76.0, s8 + 90.0)),
                           jnp.where(lt2, s8 + 112.0, s8 * 16.0 + 96.0),
                           s8 * 16.0 + 224.0], axis=0)
    big = 1024.0
    sub_s = lax.broadcasted_iota(I32, (nk, ns), 0).astype(F32)
    hd = q_ref.shape[1] // PEER_HEADS
    half = hd // 2
    tokens_per_head = tq // PEER_HEADS

    @pl.when(pl.program_id(0) == 0)
    def _():
        tes[...] = jnp.zeros_like(tes)
        tgs[...] = jnp.zeros_like(tgs)

    def top16(s, vals_ref, idx_ref):
        val = [s[8 * v:8 * v + 8, :] for v in range(nk // 8)]
        idx = [s8 + float(8 * v) for v in range(nk // 8)]
        for (a, b) in _SORT16_PAIRS:
            va, vb, ia, ib = val[a], val[b], idx[a], idx[b]
            first = (va > vb) | ((va == vb) & (ia < ib))
            val[a], val[b] = jnp.maximum(va, vb), jnp.minimum(va, vb)
            idx[a], idx[b] = jnp.where(first, ia, ib), jnp.where(first, ib, ia)
        for k in range(PEER_TOPK):
            m = jnp.max(val[0], axis=0, keepdims=True)
            i = jnp.min(jnp.where(val[0] == m, idx[0], big), axis=0, keepdims=True)
            vals_ref[k:k + 1, :] = m
            idx_ref[k:k + 1, :] = i
            win = idx[0] == i
            for d in range(PEER_TOPK - 1 - k):
                val[d] = jnp.where(win, val[d + 1], val[d])
                idx[d] = jnp.where(win, idx[d + 1], idx[d])

    def route_head(h):
        q1 = q_ref[:, h * hd:h * hd + half]
        q2 = q_ref[:, h * hd + half:(h + 1) * hd]
        nt = (((1,), (1,)), ((), ()))
        s1 = lax.dot_general(k1_ref[...], q1, nt, precision=lax.Precision.HIGHEST,
                             preferred_element_type=F32)
        s2 = lax.dot_general(k2_ref[...], q2, nt, precision=lax.Precision.HIGHEST,
                             preferred_element_type=F32)
        top16(s1, v1s, x1s)
        top16(s2, v2s, x2s)

        v2a, v2b = v2s[0:8, :], v2s[8:16, :]
        x2a, x2b = x2s[0:8, :], x2s[8:16, :]
        r1v = pltpu.roll(v1s[8:16, :], 2, 0)
        r1x = pltpu.roll(x1s[8:16, :], 2, 0)
        v1r = lambda p: v1s[p:p + 1, :]
        x1r = lambda p: x1s[p:p + 1, :]
        pieces = [
            (v1r(0) + v2a, x1r(0), x2a),
            (v1r(0) + v2b, x1r(0), x2b),
            (v1r(1) + v2a, x1r(1), x2a),
            (jnp.where(lt5, v1r(2), v1r(4)) + jnp.where(lt5, v2a, pltpu.roll(v2a, 5, 0)),
             jnp.where(lt5, x1r(2), x1r(4)), jnp.where(lt5, x2a, pltpu.roll(x2a, 5, 0))),
            (jnp.where(lt4, v1r(3), jnp.where(lt6, v1r(5), v1r(6)))
             + jnp.where(lt4, v2a, jnp.where(lt6, pltpu.roll(v2a, 4, 0), pltpu.roll(v2a, 6, 0))),
             jnp.where(lt4, x1r(3), jnp.where(lt6, x1r(5), x1r(6))),
             jnp.where(lt4, x2a, jnp.where(lt6, pltpu.roll(x2a, 4, 0), pltpu.roll(x2a, 6, 0)))),
            (jnp.where(lt2, v1r(7), r1v) + jnp.where(lt2, v2a, v2s[0:1, :]),
             jnp.where(lt2, x1r(7), r1x), jnp.where(lt2, x2a, x2s[0:1, :])),
            (jnp.where(lt2, r1v + v2s[0:1, :], neg), r1x, jnp.broadcast_to(x2s[0:1, :], (8, tq))),
        ]
        cand = jnp.concatenate([pc[0] for pc in pieces], axis=0)
        ceid = jnp.concatenate([jnp.broadcast_to(pc[1], (8, tq)) * float(nk) + pc[2] for pc in pieces], axis=0)

        tops = []
        for k in range(PEER_TOPK):
            m = jnp.max(cand, axis=0, keepdims=True)
            i = jnp.min(jnp.where(cand == m, cio, big), axis=0, keepdims=True)
            hit = cio == i
            r = h * PEER_TOPK + k
            es[r:r + 1, :] = jnp.sum(jnp.where(hit, ceid, 0.0), axis=0, keepdims=True)
            tops.append(m)
            cand = jnp.where(hit, neg, cand)
        ex = [jnp.exp(t - tops[0]) for t in tops]
        tot = ex[0]
        for e in ex[1:]:
            tot = tot + e
        for k in range(PEER_TOPK):
            r = h * PEER_TOPK + k
            gws[r:r + 1, :] = ex[k] / tot

    def expand_tokens(t0):
        eid_8 = tes[t0:t0 + 8, :]
        i1_8 = jnp.floor(eid_8 * (1.0 / nk))
        i2_8 = eid_8 - i1_8 * float(nk)
        gw_8 = tgs[t0:t0 + 8, :]
        gh_8 = gw_8.astype(BF16).astype(F32)
        gl_8 = gw_8 - gh_8
        for u in range(8):
            hit1 = sub_s == i1_8[u:u + 1, :]
            a2 = jnp.concatenate([jnp.where(hit1, gh_8[u:u + 1, :], 0.0),
                                  jnp.where(hit1, gl_8[u:u + 1, :], 0.0)], axis=1).astype(BF16)
            b1 = jnp.where(sub_s == i2_8[u:u + 1, :], 1.0, 0.0)
            b2 = jnp.concatenate([b1, b1], axis=1).astype(BF16)
            gt = lax.dot_general(a2, b2, (((1,), (1,)), ((), ())), preferred_element_type=F32)
            row0 = (t0 + u) * rpb
            for jb in range(nk // rpb):
                o_ref[0, jb, row0:row0 + rpb, :] = gt[jb * rpb:(jb + 1) * rpb, :]

    for h in range(PEER_HEADS):
        route_head(h)
        for t0 in range(h * tokens_per_head, (h + 1) * tokens_per_head, 8):
            expand_tokens(t0)

    tes[...] = es[...].T
    tgs[...] = gws[...].T


def _route_call(q2d, k1, k2, tq, tm, rpb):
    n_tok, d_q = q2d.shape
    nk = PEER_KEYS
    ns = PEER_HEADS * PEER_TOPK
    assert n_tok % tm == 0 and tm % tq == 0 and ns == nk and tq == LANES and nk % rpb == 0
    n_tiles = n_tok // tq
    per = tm // tq

    def out_idx(i):
        p = jnp.maximum(i - 1, 0)
        return (p // per, 0, p % per, 0)

    return pl.pallas_call(
        functools.partial(_route_kernel, tq=tq, rpb=rpb),
        grid=(n_tiles + 1,),
        in_specs=[pl.BlockSpec((tq, d_q), lambda i: (jnp.minimum(i, n_tiles - 1), 0)),
                  pl.BlockSpec(k1.shape, lambda i: (0, 0)),
                  pl.BlockSpec(k2.shape, lambda i: (0, 0))],
        out_specs=pl.BlockSpec((1, nk // rpb, tq * rpb, nk), out_idx),
        out_shape=jax.ShapeDtypeStruct((n_tok // tm, nk // rpb, tm * rpb, nk), F32),
        scratch_shapes=[pltpu.VMEM((PEER_TOPK, tq), F32), pltpu.VMEM((PEER_TOPK, tq), F32),
                        pltpu.VMEM((PEER_TOPK, tq), F32), pltpu.VMEM((PEER_TOPK, tq), F32),
                        pltpu.VMEM((ns, tq), F32), pltpu.VMEM((ns, tq), F32),
                        pltpu.VMEM((tq, ns), F32), pltpu.VMEM((tq, ns), F32)],
        compiler_params=_params(("arbitrary",)),
        name="peer_route",
    )(q2d, k1, k2)


def _peer_kernel(h_ref, u_ref, v_ref, g_ref, x1_ref, mod_ref, fw_ref, o_ref,
                 *, d_model, rpb, tokens_per_batch, tm):
    i = pl.program_id(0)
    j = pl.program_id(1)

    @pl.when(j == 0)
    def _():
        o_ref[...] = jnp.zeros_like(o_ref)

    z = lax.dot_general(h_ref[...], u_ref[...], (((1,), (1,)), ((), ())), preferred_element_type=F32)
    gmat = jnp.concatenate([g_ref[0, 0, pl.ds(r, tm, stride=rpb), :] for r in range(rpb)], axis=1)
    w = (_gelu_tanh(z) * gmat).astype(BF16)
    o_ref[...] += jnp.dot(w, v_ref[...], preferred_element_type=F32)

    @pl.when(j == pl.num_programs(1) - 1)
    def _():
        b = (i * tm) // tokens_per_batch
        gate2 = mod_ref[pl.ds(b, 1), 5 * d_model:6 * d_model]
        x2 = x1_ref[...] + gate2 * o_ref[...]
        ms = jnp.mean(x2 * x2, axis=-1, keepdims=True)
        o_ref[...] = x2 * lax.rsqrt(ms + EPS) * fw_ref[...]


def _peer_call(h2, u_bf, v_bf, gmat, x1, mod, final_w, tokens_per_batch, tm, rpb):
    n_tok, d_model = h2.shape
    n_exp = v_bf.shape[0]
    nk = PEER_KEYS
    eb = rpb * nk
    assert n_tok % tm == 0 and n_exp % eb == 0 and tokens_per_batch % tm == 0
    assert gmat.shape == (n_tok // tm, n_exp // eb, tm * rpb, nk)
    kern = functools.partial(_peer_kernel, d_model=d_model, rpb=rpb,
                             tokens_per_batch=tokens_per_batch, tm=tm)
    return pl.pallas_call(
        kern,
        grid=(n_tok // tm, n_exp // eb),
        in_specs=[pl.BlockSpec((tm, d_model), lambda i, j: (i, 0)),
                  pl.BlockSpec((eb, d_model), lambda i, j: (j, 0)),
                  pl.BlockSpec((eb, d_model), lambda i, j: (j, 0)),
                  pl.BlockSpec((1, 1, tm * rpb, nk), lambda i, j: (i, j, 0, 0)),
                  pl.BlockSpec((tm, d_model), lambda i, j: (i, 0), pipeline_mode=pl.Buffered(1)),
                  pl.BlockSpec(mod.shape, lambda i, j: (0, 0)),
                  pl.BlockSpec((1, d_model), lambda i, j: (0, 0))],
        out_specs=pl.BlockSpec((tm, d_model), lambda i, j: (i, 0)),
        out_shape=jax.ShapeDtypeStruct((n_tok, d_model), F32),
        compiler_params=_params(("arbitrary", "arbitrary")),
        name="peer_experts",
    )(h2, u_bf, v_bf, gmat, x1, mod, final_w[None])


def kernel(x, c, ctx, c_ctx, w_mod, b_mod, norm1_w, norm2_w, w_in, s5_a_re, s5_a_im, s5_log_dt,
           s5_b_re, s5_b_im, s5_c_re, s5_c_im, s5_d, w_glu, b_glu, pool_w, pool_scale, proj_a,
           proj_b, w_out, peer_wq, peer_k1, peer_k2, peer_u, peer_v, final_w):
    bsz, seq, d_model = x.shape
    depth = w_mod.shape[0]
    assert depth == 1, "only the single-layer configuration is implemented"
    l = 0
    s5_width = s5_d.shape[1]
    pool_width = pool_scale.shape[1]

    mod = _mod_call(c, c_ctx, w_mod[l], b_mod[l])

    w_in_bf = w_in[l].astype(BF16)
    tm = min(256, seq)
    us, up, gates = _inproj_call(x, mod, norm1_w[l], w_in_bf,
                                 (s5_width, pool_width, 2 * d_model), None, tm, "inproj_lat")
    lc = ctx.shape[1]
    (zc,) = _inproj_call(ctx, mod, norm1_w[l], w_in_bf[:, :s5_width], (s5_width,), bsz,
                         min(256, lc), "inproj_ctx")

    disc = _disc_call(s5_a_re[l], s5_a_im[l], s5_log_dt[l], s5_b_re[l], s5_b_im[l])
    w_b, w_c, a_r, a_i = _s5_weights(*disc, s5_c_re[l], s5_c_im[l])
    chunk = min(256, lc, seq)
    y2 = _s5_call(zc, us, w_b, w_c, a_r, a_i, chunk)

    x1, h2, q = _mix_call(x, y2, us, up, gates, mod, s5_d[l], w_glu[l], b_glu[l], pool_w[l],
                          pool_scale[l], proj_a[l], proj_b[l], w_out[l], norm2_w[l], peer_wq[l], tm)

    n_tok = bsz * seq
    tm_peer, rpb = min(1024, seq), 4
    gmat = _route_call(q.reshape(n_tok, d_model), peer_k1[l], peer_k2[l], LANES, tm_peer, rpb)
    out = _peer_call(h2.reshape(n_tok, d_model), peer_u[l].astype(BF16), peer_v[l].astype(BF16),
                     gmat, x1.reshape(n_tok, d_model), mod, final_w, seq, tm_peer, rpb)
    return out.reshape(bsz, seq, d_model)
```

```python
import functools
import math
from typing import NamedTuple

import jax
import jax.numpy as jnp
from jax import lax
from jax.experimental import pallas as pl
from jax.experimental.pallas import tpu as pltpu

F32 = jnp.float32
BF16 = jnp.bfloat16
I32 = jnp.int32

LANES = 128
EPS = 1e-6
GRID_W = 64
S5_GROUPS_PER_BLOCK = 8
POOL_WINDOWS = (2, 4, 8, 16)
PEER_HEADS = 8
PEER_KEYS = 128
PEER_TOPK = 16

V7X_VMEM_BYTES = 64 * 1024 * 1024
VMEM_LIMIT = V7X_VMEM_BYTES - 8 * 1024 * 1024
PEER_VMEM_LIMIT = V7X_VMEM_BYTES - 4 * 1024 * 1024


class _Tiles(NamedTuple):
    tok: int
    ctx: int
    chunk: int
    route: int
    peer: int
    rpb: int


def _tile_plan(seq, ctx_len):
    return _Tiles(tok=min(256, seq), ctx=min(256, ctx_len), chunk=min(256, ctx_len, seq),
                  route=LANES, peer=min(1024, seq), rpb=8)


def _params(sem, vmem=VMEM_LIMIT):
    return pltpu.CompilerParams(dimension_semantics=sem, vmem_limit_bytes=vmem)


def _gelu_tanh(x):
    return 0.5 * x * (1.0 + jnp.tanh(math.sqrt(2.0 / math.pi) * (x + 0.044715 * (x * x * x))))


def _sigmoid(x):
    return 1.0 / (1.0 + jnp.exp(-x))


def _mod_kernel(cb_ref, w_ref, b_ref, o_ref, a_scr, *, n_rows):
    d_model, tn = w_ref.shape
    reps = tn // LANES

    @pl.when(pl.program_id(0) == 0)
    def _():
        cb = cb_ref[...]
        a_scr[...] = cb * _sigmoid(cb)

    def body(kb, accs):
        k0 = pl.multiple_of(kb * 8, 8)
        wt = w_ref[pl.ds(k0, 8), :]
        return tuple(acc + jnp.concatenate([a_scr[r, pl.ds(k0, 8), :]] * reps, axis=1) * wt
                     for r, acc in enumerate(accs))

    accs = lax.fori_loop(0, d_model // 8, body,
                         tuple(jnp.zeros((8, tn), F32) for _ in range(n_rows)), unroll=2)
    for r in range(8):
        if r < n_rows:
            o_ref[r:r + 1, :] = jnp.sum(accs[r], axis=0, keepdims=True) + b_ref[...]
        else:
            o_ref[r:r + 1, :] = jnp.zeros((1, tn), F32)


def _mod_call(c, c_ctx, w_mod, b_mod):
    bsz, d_model = c.shape
    n_out = w_mod.shape[1]
    n_rows = bsz + 1
    assert n_rows <= 8
    cb = jnp.broadcast_to(jnp.concatenate([c, c_ctx[None]], axis=0)[:, :, None], (n_rows, d_model, LANES))
    tn = 512
    return pl.pallas_call(
        functools.partial(_mod_kernel, n_rows=n_rows),
        grid=(n_out // tn,),
        in_specs=[pl.BlockSpec((n_rows, d_model, LANES), lambda j: (0, 0, 0)),
                  pl.BlockSpec((d_model, tn), lambda j: (0, j)),
                  pl.BlockSpec((1, tn), lambda j: (0, j))],
        out_specs=pl.BlockSpec((8, tn), lambda j: (0, j)),
        out_shape=jax.ShapeDtypeStruct((8, n_out), F32),
        scratch_shapes=[pltpu.VMEM((n_rows, d_model, LANES), F32)],
        compiler_params=_params(("arbitrary",)),
        name="mod",
    )(cb, w_mod, b_mod[None])


def _modulated_norm(x, nw, shift, scale):
    ms = jnp.mean(x * x, axis=-1, keepdims=True)
    y = x * lax.rsqrt(ms + EPS) * nw
    return y * (1.0 + scale) + shift


def _inproj_kernel(x_ref, mod_ref, nw_ref, w_ref, *o_refs, d_model, widths, ctx_row):
    row = ctx_row if ctx_row is not None else pl.program_id(0)
    shift = mod_ref[pl.ds(row, 1), 0:d_model]
    scale = mod_ref[pl.ds(row, 1), d_model:2 * d_model]
    h = _modulated_norm(x_ref[0], nw_ref[...], shift, scale).astype(BF16)
    c0 = 0
    for o_ref, wd in zip(o_refs, widths):
        o_ref[0] = jnp.dot(h, w_ref[:, c0:c0 + wd], preferred_element_type=F32).astype(o_ref.dtype)
        c0 += wd


def _inproj_call(x, mod, norm_w, w_bf, widths, ctx_row, tm, name):
    bsz, seq, d_model = x.shape
    n_in = w_bf.shape[1]
    assert sum(widths) == n_in and seq % tm == 0
    kern = functools.partial(_inproj_kernel, d_model=d_model, widths=widths, ctx_row=ctx_row)
    return pl.pallas_call(
        kern,
        grid=(bsz, seq // tm),
        in_specs=[pl.BlockSpec((1, tm, d_model), lambda b, i: (b, i, 0)),
                  pl.BlockSpec(mod.shape, lambda b, i: (0, 0)),
                  pl.BlockSpec((1, d_model), lambda b, i: (0, 0)),
                  pl.BlockSpec((d_model, n_in), lambda b, i: (0, 0))],
        out_specs=[pl.BlockSpec((1, tm, wd), lambda b, i: (b, i, 0)) for wd in widths],
        out_shape=[jax.ShapeDtypeStruct((bsz, seq, wd), BF16) for wd in widths],
        compiler_params=_params(("arbitrary", "arbitrary")),
        name=name,
    )(x, mod, norm_w[None], w_bf)


def _disc_kernel(are_ref, aim_ref, ldt_ref, bre_ref, bim_ref,
                 abr_ref, abi_ref, bbr_ref, bbi_ref, bar_ref, bai_ref):
    a_re = are_ref[...]
    a_im = aim_ref[...]
    dt = jnp.exp(ldt_ref[...])
    mag = jnp.exp(dt * a_re)
    ab_re = mag * jnp.cos(dt * a_im)
    ab_im = mag * jnp.sin(dt * a_im)
    den = a_re * a_re + a_im * a_im
    nr = ab_re - 1.0
    ni = ab_im
    f_re = (nr * a_re + ni * a_im) / den
    f_im = (ni * a_re - nr * a_im) / den
    b_re = bre_ref[...]
    b_im = bim_ref[...]
    bb_re = f_re * b_re - f_im * b_im
    bb_im = f_re * b_im + f_im * b_re
    abr_ref[...] = ab_re
    abi_ref[...] = ab_im
    bbr_ref[...] = bb_re
    bbi_ref[...] = bb_im
    bar_ref[...] = ab_re * bb_re - ab_im * bb_im
    bai_ref[...] = ab_re * bb_im + ab_im * bb_re


def _disc_call(a_re, a_im, log_dt, b_re, b_im):
    nd, g, p, h = b_re.shape
    rows, cols = nd * g, p * h
    rep = lambda a: jnp.broadcast_to(a[..., None], (nd, g, p, h)).reshape(rows, cols)
    args = (rep(a_re), rep(a_im),
            jnp.broadcast_to(log_dt[..., None, None], (nd, g, p, h)).reshape(rows, cols),
            b_re.reshape(rows, cols), b_im.reshape(rows, cols))
    outs = pl.pallas_call(
        _disc_kernel,
        out_shape=[jax.ShapeDtypeStruct((rows, cols), F32)] * 6,
        name="s5_disc",
    )(*args)
    ab_re, ab_im, bb_re, bb_im, ba_re, ba_im = (o.reshape(nd, g, p, h) for o in outs)
    return ab_re[..., 0], ab_im[..., 0], bb_re, bb_im, ba_re, ba_im


def _s5_weights(ab_re, ab_im, bb_re, bb_im, ba_re, ba_im, c_re, c_im):
    nd, g, p, h = bb_re.shape
    gb = S5_GROUPS_PER_BLOCK
    nb = g // gb
    eye = jnp.eye(gb, dtype=bool)

    def in_mat(bb):
        t = bb.reshape(nd, nb, gb, p, h).transpose(0, 1, 2, 4, 3)
        t = jnp.where(eye[None, None, :, None, :, None], t[:, :, :, :, None, :], 0.0)
        return t.reshape(nd, nb, gb * h, gb * p)

    def out_mat(cc):
        t = cc.reshape(nd, nb, gb, h, p).transpose(0, 1, 2, 4, 3)
        t = jnp.where(eye[None, None, :, None, :, None], t[:, :, :, :, None, :], 0.0)
        return t.reshape(nd, nb, gb * p, gb * h)

    w_own = jnp.concatenate([in_mat(bb_re), in_mat(bb_im)], axis=-1)
    w_par = jnp.concatenate([in_mat(ba_re), in_mat(ba_im)], axis=-1)
    w_b = jnp.concatenate([w_own, w_par], axis=-2).astype(BF16)
    w_c = jnp.concatenate([out_mat(c_re.astype(F32)), -out_mat(c_im.astype(F32))], axis=-2).astype(BF16)
    a_r = ab_re.reshape(nd, nb, gb * p)
    a_i = ab_im.reshape(nd, nb, gb * p)
    return w_b, w_c, a_r, a_i


S5_ROWS = 4


def _s5_kernel(zc_ref, zl_ref, wb_ref, wc_ref, ar_ref, ai_ref, y_ref, sbuf, h_ref,
               *, n_ctx_chunks, chunk, nb, half, nbat):
    d = pl.program_id(1)
    c = pl.program_id(2)
    cw = 2 * half
    bw = wb_ref.shape[2] // 2
    nl = cw // LANES
    hl = half // LANES
    nhb = nb // S5_ROWS
    npair = chunk // 2

    @pl.when(c == 0)
    def _():
        h_ref[...] = jnp.zeros_like(h_ref)

    sub8 = lax.broadcasted_iota(I32, (8, LANES), 0)
    second = (sub8 >= S5_ROWS) == (d == 0)
    row_par = lax.broadcasted_iota(I32, (chunk, 1), 0) % 2
    is_second = row_par == (1 - d)

    def project(src_ref):
        for e in range(nbat):
            x = src_ref[e]
            xf = x.astype(F32)
            partner = jnp.where(d == 0, pltpu.roll(xf, 1, 0), pltpu.roll(xf, chunk - 1, 0))
            xp = jnp.where(is_second, partner, 0.0).astype(BF16)
            for k in range(nb):
                hb, kk = divmod(k, S5_ROWS)
                lhs = jnp.concatenate([x[:, k * bw:(k + 1) * bw], xp[:, k * bw:(k + 1) * bw]], axis=1)
                bu = jnp.dot(lhs, wb_ref[0, k], preferred_element_type=F32)
                for l in range(nl):
                    sbuf[e, hb * nl + l, pl.ds(kk, chunk, stride=S5_ROWS), :] = bu[:, l * LANES:(l + 1) * LANES]

    def scan():
        p_r, p_i = [], []
        for hb in range(nhb):
            for l in range(hl):
                a_r = ar_ref[0, hb * S5_ROWS:(hb + 1) * S5_ROWS, l * LANES:(l + 1) * LANES]
                a_i = ai_ref[0, hb * S5_ROWS:(hb + 1) * S5_ROWS, l * LANES:(l + 1) * LANES]
                a_r = jnp.concatenate([a_r, a_r], axis=0)
                a_i = jnp.concatenate([a_i, a_i], axis=0)
                p_r.append(jnp.where(second, a_r * a_r - a_i * a_i, a_r))
                p_i.append(jnp.where(second, 2.0 * a_r * a_i, a_i))
        nch = nhb * hl

        for e in range(nbat):
            def step(s, carry):
                m = s + d * (npair - 1 - 2 * s)
                row = pl.multiple_of(m * 8, 8)
                out = []
                for hb in range(nhb):
                    for l in range(hl):
                        ch = hb * hl + l
                        c_r, c_i = carry[2 * ch], carry[2 * ch + 1]
                        pr, pi = hb * nl + l, hb * nl + hl + l
                        n_r = p_r[ch] * c_r - p_i[ch] * c_i + sbuf[e, pr, pl.ds(row, 8), :]
                        n_i = p_r[ch] * c_i + p_i[ch] * c_r + sbuf[e, pi, pl.ds(row, 8), :]
                        sbuf[e, pr, pl.ds(row, 8), :] = n_r
                        sbuf[e, pi, pl.ds(row, 8), :] = n_i
                        out += [jnp.where(second, n_r, pltpu.roll(n_r, S5_ROWS, 0)),
                                jnp.where(second, n_i, pltpu.roll(n_i, S5_ROWS, 0))]
                return tuple(out)

            init = []
            for ch in range(nch):
                init += [h_ref[e, 2 * ch], h_ref[e, 2 * ch + 1]]
            fin = lax.fori_loop(0, npair, step, tuple(init), unroll=2)
            for ch in range(nch):
                h_ref[e, 2 * ch] = fin[2 * ch]
                h_ref[e, 2 * ch + 1] = fin[2 * ch + 1]

    def readout():
        for e in range(nbat):
            for k in range(nb):
                hb, kk = divmod(k, S5_ROWS)
                hk = jnp.concatenate([sbuf[e, hb * nl + l, pl.ds(kk, chunk, stride=S5_ROWS), :]
                                      for l in range(nl)], axis=1)
                y_ref[0, e, :, k * bw:(k + 1) * bw] = jnp.dot(hk.astype(BF16), wc_ref[0, k],
                                                              preferred_element_type=F32)

    @pl.when(c < n_ctx_chunks)
    def _():
        project(zc_ref)
        scan()

    @pl.when(c >= n_ctx_chunks)
    def _():
        project(zl_ref)
        scan()
        readout()


def _s5_call(zc, zl, w_b, w_c, a_r, a_i, chunk):
    bsz, lc, width = zc.shape
    seq = zl.shape[1]
    nd, nb, bw2, cw = w_b.shape
    bw = bw2 // 2
    half = cw // 2
    assert nb % S5_ROWS == 0 and lc % chunk == 0 and seq % chunk == 0 and width == nb * bw and chunk % 8 == 0
    ncc, nlc = lc // chunk, seq // chunk
    nbat = 2 if bsz % 2 == 0 else 1
    nplanes = (nb // S5_ROWS) * (cw // LANES)

    def ctx_idx(b, d, c):
        cc = jnp.minimum(c, ncc - 1)
        return (b, cc + d * (ncc - 1 - 2 * cc), 0)

    def lat_chunk(d, c):
        lc_ = jnp.maximum(c - ncc, 0)
        return lc_ + d * (nlc - 1 - 2 * lc_)

    kern = functools.partial(_s5_kernel, n_ctx_chunks=ncc, chunk=chunk, nb=nb, half=half, nbat=nbat)
    return pl.pallas_call(
        kern,
        grid=(bsz // nbat, nd, ncc + nlc),
        in_specs=[pl.BlockSpec((nbat, chunk, width), ctx_idx),
                  pl.BlockSpec((nbat, chunk, width), lambda b, d, c: (b, lat_chunk(d, c), 0)),
                  pl.BlockSpec((1, nb, bw2, cw), lambda b, d, c: (d, 0, 0, 0)),
                  pl.BlockSpec((1, nb, cw, bw), lambda b, d, c: (d, 0, 0, 0)),
                  pl.BlockSpec((1, nb, half), lambda b, d, c: (d, 0, 0)),
                  pl.BlockSpec((1, nb, half), lambda b, d, c: (d, 0, 0))],
        out_specs=pl.BlockSpec((1, nbat, chunk, width), lambda b, d, c: (d, b, lat_chunk(d, c), 0)),
        out_shape=jax.ShapeDtypeStruct((nd, bsz, seq, width), F32),
        scratch_shapes=[pltpu.VMEM((nbat, nplanes, chunk * S5_ROWS, LANES), F32),
                        pltpu.VMEM((nbat, nplanes, 8, LANES), F32)],
        compiler_params=_params(("arbitrary", "arbitrary", "arbitrary")),
        name="s5_scan",
    )(zc, zl, w_b, w_c, a_r, a_i)


def _mix_kernel(x_ref, yf_ref, yb_ref, us_ref, up_ref, gt_ref, mod_ref, dsk_ref, wglu_ref, bglu_ref,
                pw_ref, psc_ref, pa_ref, pb_ref, wo_ref, nw2_ref, wq_ref,
                x1_ref, h2_ref, q_ref, *, d_model, tm):
    b = pl.program_id(0)
    gate1 = mod_ref[pl.ds(b, 1), 2 * d_model:3 * d_model]
    shift2 = mod_ref[pl.ds(b, 1), 3 * d_model:4 * d_model]
    scale2 = mod_ref[pl.ds(b, 1), 4 * d_model:5 * d_model]

    u_s = us_ref[0].astype(F32)
    y = yf_ref[0, 0] + yb_ref[0, 0] + dsk_ref[...] * u_s
    ge = _gelu_tanh(y)
    lin = jnp.dot(ge.astype(BF16), wglu_ref[...], preferred_element_type=F32) + bglu_ref[...]
    y_a = (ge * _sigmoid(lin)).astype(BF16)

    u_p = up_ref[0]
    ri = lax.broadcasted_iota(I32, (tm, tm), 0)
    ci = lax.broadcasted_iota(I32, (tm, tm), 1)
    same_row = (ri // GRID_W) == (ci // GRID_W)
    pos_r = ri % GRID_W
    pos_c = ci % GRID_W
    pos1 = lax.broadcasted_iota(I32, (tm, 1), 0) % GRID_W
    gw = pw_ref.shape[1]
    y_b_parts = []
    for gi, w in enumerate(POOL_WINDOWS):
        lo = jnp.maximum(pos_r - w // 2, 0)
        hi = jnp.minimum(pos_r + w // 2 - 1, GRID_W - 1)
        band = jnp.where(same_row & (pos_c >= lo) & (pos_c <= hi), 1.0, 0.0).astype(BF16)
        ug = u_p[:, gi * gw:(gi + 1) * gw]
        win = jnp.dot(band, ug, preferred_element_type=F32)
        cnt = (jnp.minimum(pos1 + w // 2 - 1, GRID_W - 1) - jnp.maximum(pos1 - w // 2, 0) + 1).astype(F32)
        p = win / cnt - ug.astype(F32)
        y_b_parts.append(jnp.dot(p.astype(BF16), pw_ref[gi], preferred_element_type=F32))
    y_b = (jnp.concatenate(y_b_parts, axis=-1) * psc_ref[...]).astype(BF16)

    g_a = _sigmoid(gt_ref[0, :, 0:d_model].astype(F32))
    g_b = _sigmoid(gt_ref[0, :, d_model:2 * d_model].astype(F32))
    m = (g_a * jnp.dot(y_a, pa_ref[...], preferred_element_type=F32)
         + g_b * jnp.dot(y_b, pb_ref[...], preferred_element_type=F32))
    x1 = x_ref[0] + gate1 * jnp.dot(m.astype(BF16), wo_ref[...], preferred_element_type=F32)
    x1_ref[0] = x1

    h2 = _modulated_norm(x1, nw2_ref[...], shift2, scale2).astype(BF16)
    h2_ref[0] = h2
    q_ref[0] = jnp.dot(h2, wq_ref[...], preferred_element_type=F32)


def _mix_call(x, y2, us, up, gates, mod, d_skip, w_glu, b_glu, pool_w, pool_scale,
              proj_a, proj_b, w_out, norm2_w, w_q, tm):
    bsz, seq, d_model = x.shape
    sw = us.shape[2]
    pwid = up.shape[2]
    assert seq % tm == 0 and tm % GRID_W == 0
    full = lambda a: pl.BlockSpec(a.shape, lambda b, i: (0,) * a.ndim)
    row = lambda a: a[None]
    tok = lambda wd: pl.BlockSpec((1, tm, wd), lambda b, i: (b, i, 0))
    args = (x, y2, y2, us, up, gates, mod, row(d_skip), w_glu.astype(BF16), row(b_glu),
            pool_w.astype(BF16), row(pool_scale), proj_a.astype(BF16), proj_b.astype(BF16),
            w_out.astype(BF16), row(norm2_w), w_q.astype(BF16))
    in_specs = [tok(d_model),
                pl.BlockSpec((1, 1, tm, sw), lambda b, i: (0, b, i, 0)),
                pl.BlockSpec((1, 1, tm, sw), lambda b, i: (1, b, i, 0)),
                tok(sw), tok(pwid), tok(2 * d_model)] + [full(a) for a in args[6:]]
    return pl.pallas_call(
        functools.partial(_mix_kernel, d_model=d_model, tm=tm),
        grid=(bsz, seq // tm),
        in_specs=in_specs,
        out_specs=[tok(d_model), tok(d_model), tok(d_model)],
        out_shape=[jax.ShapeDtypeStruct((bsz, seq, d_model), F32),
                   jax.ShapeDtypeStruct((bsz, seq, d_model), BF16),
                   jax.ShapeDtypeStruct((bsz, seq, d_model), F32)],
        compiler_params=_params(("arbitrary", "arbitrary")),
        name="mix",
    )(*args)


def _oddeven_merge_sort_pairs(n):
    pairs = []

    def merge(lo, hi, r):
        step = r * 2
        if step < hi - lo:
            merge(lo, hi, step)
            merge(lo + r, hi, step)
            pairs.extend((i, i + r) for i in range(lo + r, hi - r, step))
        else:
            pairs.append((lo, lo + r))

    def sort(lo, hi):
        if hi > lo:
            mid = lo + (hi - lo) // 2
            sort(lo, mid)
            sort(mid + 1, hi)
            merge(lo, hi, 1)

    sort(0, n - 1)
    return pairs


_SORT16_PAIRS = _oddeven_merge_sort_pairs(PEER_KEYS // 8)


def _route_kernel(q_ref, k1_ref, k2_ref, o_ref,
                  v1s, x1s, v2s, x2s, es, gws, tes, tgs, *, tq, rpb):
    neg = -jnp.inf
    nk = PEER_KEYS
    ns = PEER_HEADS * PEER_TOPK
    sub8 = lax.broadcasted_iota(I32, (8, tq), 0)
    lt2, lt4, lt5, lt6 = sub8 < 2, sub8 < 4, sub8 < 5, sub8 < 6
    s8 = sub8.astype(F32)
    cio = jnp.concatenate([s8, s8 + 8.0, s8 + 16.0,
                           jnp.where(lt5, s8 + 32.0, s8 + 59.0),
                           jnp.where(lt4, s8 + 48.0, jnp.where(lt6, s8 + 76.0, s8 + 90.0)),
                           jnp.where(lt2, s8 + 112.0, s8 * 16.0 + 96.0),
                           s8 * 16.0 + 224.0], axis=0)
    big = 1024.0
    sub_s = lax.broadcasted_iota(I32, (nk, ns), 0).astype(F32)
    hd = q_ref.shape[1] // PEER_HEADS
    half = hd // 2
    tokens_per_head = tq // PEER_HEADS

    @pl.when(pl.program_id(0) == 0)
    def _():
        tes[...] = jnp.zeros_like(tes)
        tgs[...] = jnp.zeros_like(tgs)

    def top16(s, vals_ref, idx_ref):
        val = [s[8 * v:8 * v + 8, :] for v in range(nk // 8)]
        idx = [s8 + float(8 * v) for v in range(nk // 8)]
        for (a, b) in _SORT16_PAIRS:
            va, vb, ia, ib = val[a], val[b], idx[a], idx[b]
            first = (va > vb) | ((va == vb) & (ia < ib))
            val[a], val[b] = jnp.maximum(va, vb), jnp.minimum(va, vb)
            idx[a], idx[b] = jnp.where(first, ia, ib), jnp.where(first, ib, ia)
        for k in range(PEER_TOPK):
            m = jnp.max(val[0], axis=0, keepdims=True)
            i = jnp.min(jnp.where(val[0] == m, idx[0], big), axis=0, keepdims=True)
            vals_ref[k:k + 1, :] = m
            idx_ref[k:k + 1, :] = i
            win = idx[0] == i
            for d in range(PEER_TOPK - 1 - k):
                val[d] = jnp.where(win, val[d + 1], val[d])
                idx[d] = jnp.where(win, idx[d + 1], idx[d])

    def route_head(h):
        q1 = q_ref[:, h * hd:h * hd + half]
        q2 = q_ref[:, h * hd + half:(h + 1) * hd]
        nt = (((1,), (1,)), ((), ()))
        s1 = lax.dot_general(k1_ref[...], q1, nt, precision=lax.Precision.HIGHEST,
                             preferred_element_type=F32)
        s2 = lax.dot_general(k2_ref[...], q2, nt, precision=lax.Precision.HIGHEST,
                             preferred_element_type=F32)
        top16(s1, v1s, x1s)
        top16(s2, v2s, x2s)

        v2a, v2b = v2s[0:8, :], v2s[8:16, :]
        x2a, x2b = x2s[0:8, :], x2s[8:16, :]
        r1v = pltpu.roll(v1s[8:16, :], 2, 0)
        r1x = pltpu.roll(x1s[8:16, :], 2, 0)
        v1r = lambda p: v1s[p:p + 1, :]
        x1r = lambda p: x1s[p:p + 1, :]
        pieces = [
            (v1r(0) + v2a, x1r(0), x2a),
            (v1r(0) + v2b, x1r(0), x2b),
            (v1r(1) + v2a, x1r(1), x2a),
            (jnp.where(lt5, v1r(2), v1r(4)) + jnp.where(lt5, v2a, pltpu.roll(v2a, 5, 0)),
             jnp.where(lt5, x1r(2), x1r(4)), jnp.where(lt5, x2a, pltpu.roll(x2a, 5, 0))),
            (jnp.where(lt4, v1r(3), jnp.where(lt6, v1r(5), v1r(6)))
             + jnp.where(lt4, v2a, jnp.where(lt6, pltpu.roll(v2a, 4, 0), pltpu.roll(v2a, 6, 0))),
             jnp.where(lt4, x1r(3), jnp.where(lt6, x1r(5), x1r(6))),
             jnp.where(lt4, x2a, jnp.where(lt6, pltpu.roll(x2a, 4, 0), pltpu.roll(x2a, 6, 0)))),
            (jnp.where(lt2, v1r(7), r1v) + jnp.where(lt2, v2a, v2s[0:1, :]),
             jnp.where(lt2, x1r(7), r1x), jnp.where(lt2, x2a, x2s[0:1, :])),
            (jnp.where(lt2, r1v + v2s[0:1, :], neg), r1x, jnp.broadcast_to(x2s[0:1, :], (8, tq))),
        ]
        cand = jnp.concatenate([pc[0] for pc in pieces], axis=0)
        ceid = jnp.concatenate([jnp.broadcast_to(pc[1], (8, tq)) * float(nk) + pc[2] for pc in pieces], axis=0)

        tops = []
        for k in range(PEER_TOPK):
            m = jnp.max(cand, axis=0, keepdims=True)
            i = jnp.min(jnp.where(cand == m, cio, big), axis=0, keepdims=True)
            hit = cio == i
            r = h * PEER_TOPK + k
            es[r:r + 1, :] = jnp.sum(jnp.where(hit, ceid, 0.0), axis=0, keepdims=True)
            tops.append(m)
            cand = jnp.where(hit, neg, cand)
        ex = [jnp.exp(t - tops[0]) for t in tops]
        tot = ex[0]
        for e in ex[1:]:
            tot = tot + e
        for k in range(PEER_TOPK):
            r = h * PEER_TOPK + k
            gws[r:r + 1, :] = ex[k] / tot

    def expand_tokens(t0):
        eid_8 = tes[t0:t0 + 8, :]
        i1_8 = jnp.floor(eid_8 * (1.0 / nk))
        i2_8 = eid_8 - i1_8 * float(nk)
        gw_8 = tgs[t0:t0 + 8, :]
        gh_8 = gw_8.astype(BF16).astype(F32)
        gl_8 = gw_8 - gh_8
        for u in range(8):
            hit1 = sub_s == i1_8[u:u + 1, :]
            a2 = jnp.concatenate([jnp.where(hit1, gh_8[u:u + 1, :], 0.0),
                                  jnp.where(hit1, gl_8[u:u + 1, :], 0.0)], axis=1).astype(BF16)
            b1 = jnp.where(sub_s == i2_8[u:u + 1, :], 1.0, 0.0)
            b2 = jnp.concatenate([b1, b1], axis=1).astype(BF16)
            gt = lax.dot_general(a2, b2, (((1,), (1,)), ((), ())), preferred_element_type=F32)
            row0 = (t0 + u) * rpb
            for jb in range(nk // rpb):
                o_ref[0, jb, row0:row0 + rpb, :] = gt[jb * rpb:(jb + 1) * rpb, :]

    for h in range(PEER_HEADS):
        route_head(h)
        for t0 in range(h * tokens_per_head, (h + 1) * tokens_per_head, 8):
            expand_tokens(t0)

    tes[...] = es[...].T
    tgs[...] = gws[...].T


def _route_call(q2d, k1, k2, tq, tm, rpb):
    n_tok, d_q = q2d.shape
    nk = PEER_KEYS
    ns = PEER_HEADS * PEER_TOPK
    assert n_tok % tm == 0 and tm % tq == 0 and ns == nk and tq == LANES and nk % rpb == 0
    n_tiles = n_tok // tq
    per = tm // tq

    def out_idx(i):
        p = jnp.maximum(i - 1, 0)
        return (p // per, 0, p % per, 0)

    return pl.pallas_call(
        functools.partial(_route_kernel, tq=tq, rpb=rpb),
        grid=(n_tiles + 1,),
        in_specs=[pl.BlockSpec((tq, d_q), lambda i: (jnp.minimum(i, n_tiles - 1), 0)),
                  pl.BlockSpec(k1.shape, lambda i: (0, 0)),
                  pl.BlockSpec(k2.shape, lambda i: (0, 0))],
        out_specs=pl.BlockSpec((1, nk // rpb, tq * rpb, nk), out_idx),
        out_shape=jax.ShapeDtypeStruct((n_tok // tm, nk // rpb, tm * rpb, nk), F32),
        scratch_shapes=[pltpu.VMEM((PEER_TOPK, tq), F32), pltpu.VMEM((PEER_TOPK, tq), F32),
                        pltpu.VMEM((PEER_TOPK, tq), F32), pltpu.VMEM((PEER_TOPK, tq), F32),
                        pltpu.VMEM((ns, tq), F32), pltpu.VMEM((ns, tq), F32),
                        pltpu.VMEM((tq, ns), F32), pltpu.VMEM((tq, ns), F32)],
        compiler_params=_params(("arbitrary",)),
        name="peer_route",
    )(q2d, k1, k2)


def _peer_kernel(h_ref, u_ref, v_ref, g_ref, x1_ref, mod_ref, fw_ref, o_ref,
                 *, d_model, rpb, tokens_per_batch, tm):
    i = pl.program_id(0)
    j = pl.program_id(1)

    @pl.when(j == 0)
    def _():
        o_ref[...] = jnp.zeros_like(o_ref)

    z = lax.dot_general(h_ref[...], u_ref[...], (((1,), (1,)), ((), ())), preferred_element_type=F32)
    gmat = jnp.concatenate([g_ref[0, 0, pl.ds(r, tm, stride=rpb), :] for r in range(rpb)], axis=1)
    w = (_gelu_tanh(z) * gmat).astype(BF16)
    o_ref[...] += jnp.dot(w, v_ref[...], preferred_element_type=F32)

    @pl.when(j == pl.num_programs(1) - 1)
    def _():
        b = (i * tm) // tokens_per_batch
        gate2 = mod_ref[pl.ds(b, 1), 5 * d_model:6 * d_model]
        x2 = x1_ref[...] + gate2 * o_ref[...]
        ms = jnp.mean(x2 * x2, axis=-1, keepdims=True)
        o_ref[...] = x2 * lax.rsqrt(ms + EPS) * fw_ref[...]


def _peer_call(h2, u_bf, v_bf, gmat, x1, mod, final_w, tokens_per_batch, tm, rpb):
    n_tok, d_model = h2.shape
    n_exp = v_bf.shape[0]
    nk = PEER_KEYS
    eb = rpb * nk
    assert n_tok % tm == 0 and n_exp % eb == 0 and tokens_per_batch % tm == 0
    assert gmat.shape == (n_tok // tm, n_exp // eb, tm * rpb, nk)
    kern = functools.partial(_peer_kernel, d_model=d_model, rpb=rpb,
                             tokens_per_batch=tokens_per_batch, tm=tm)
    once = pl.Buffered(1)
    return pl.pallas_call(
        kern,
        grid=(n_tok // tm, n_exp // eb),
        in_specs=[pl.BlockSpec((tm, d_model), lambda i, j: (i, 0), pipeline_mode=once),
                  pl.BlockSpec((eb, d_model), lambda i, j: (j, 0)),
                  pl.BlockSpec((eb, d_model), lambda i, j: (j, 0)),
                  pl.BlockSpec((1, 1, tm * rpb, nk), lambda i, j: (i, j, 0, 0)),
                  pl.BlockSpec((tm, d_model), lambda i, j: (i, 0), pipeline_mode=once),
                  pl.BlockSpec(mod.shape, lambda i, j: (0, 0)),
                  pl.BlockSpec((1, d_model), lambda i, j: (0, 0))],
        out_specs=pl.BlockSpec((tm, d_model), lambda i, j: (i, 0), pipeline_mode=once),
        out_shape=jax.ShapeDtypeStruct((n_tok, d_model), F32),
        compiler_params=_params(("arbitrary", "arbitrary"), PEER_VMEM_LIMIT),
        name="peer_experts",
    )(h2, u_bf, v_bf, gmat, x1, mod, final_w[None])


def kernel(x, c, ctx, c_ctx, w_mod, b_mod, norm1_w, norm2_w, w_in, s5_a_re, s5_a_im, s5_log_dt,
           s5_b_re, s5_b_im, s5_c_re, s5_c_im, s5_d, w_glu, b_glu, pool_w, pool_scale, proj_a,
           proj_b, w_out, peer_wq, peer_k1, peer_k2, peer_u, peer_v, final_w):
    bsz, seq, d_model = x.shape
    depth = w_mod.shape[0]
    assert depth == 1, "only the single-layer configuration is implemented"
    l = 0
    s5_width = s5_d.shape[1]
    pool_width = pool_scale.shape[1]

    mod = _mod_call(c, c_ctx, w_mod[l], b_mod[l])

    tiles = _tile_plan(seq, ctx.shape[1])
    w_in_bf = w_in[l].astype(BF16)
    us, up, gates = _inproj_call(x, mod, norm1_w[l], w_in_bf,
                                 (s5_width, pool_width, 2 * d_model), None, tiles.tok, "inproj_lat")
    (zc,) = _inproj_call(ctx, mod, norm1_w[l], w_in_bf[:, :s5_width], (s5_width,), bsz,
                         tiles.ctx, "inproj_ctx")

    disc = _disc_call(s5_a_re[l], s5_a_im[l], s5_log_dt[l], s5_b_re[l], s5_b_im[l])
    w_b, w_c, a_r, a_i = _s5_weights(*disc, s5_c_re[l], s5_c_im[l])
    y2 = _s5_call(zc, us, w_b, w_c, a_r, a_i, tiles.chunk)

    x1, h2, q = _mix_call(x, y2, us, up, gates, mod, s5_d[l], w_glu[l], b_glu[l], pool_w[l],
                          pool_scale[l], proj_a[l], proj_b[l], w_out[l], norm2_w[l], peer_wq[l], tiles.tok)

    n_tok = bsz * seq
    gmat = _route_call(q.reshape(n_tok, d_model), peer_k1[l], peer_k2[l], tiles.route, tiles.peer, tiles.rpb)
    out = _peer_call(h2.reshape(n_tok, d_model), peer_u[l].astype(BF16), peer_v[l].astype(BF16),
                     gmat, x1.reshape(n_tok, d_model), mod, final_w, seq, tiles.peer, tiles.rpb)
    return out.reshape(bsz, seq, d_model)
```

```python
import functools
import math
from typing import NamedTuple

import jax
import jax.numpy as jnp
from jax import lax
from jax.experimental import pallas as pl
from jax.experimental.pallas import tpu as pltpu

F32 = jnp.float32
BF16 = jnp.bfloat16
I32 = jnp.int32

LANES = 128
EPS = 1e-6
GRID_W = 64
S5_GROUPS_PER_BLOCK = 8
POOL_WINDOWS = (2, 4, 8, 16)
PEER_HEADS = 8
PEER_KEYS = 128
PEER_TOPK = 16

V7X_VMEM_BYTES = 64 * 1024 * 1024
VMEM_LIMIT = V7X_VMEM_BYTES - 8 * 1024 * 1024
PEER_VMEM_LIMIT = V7X_VMEM_BYTES - 4 * 1024 * 1024


class _Tiles(NamedTuple):
    tok: int
    ctx: int
    chunk: int
    route: int
    peer: int
    rpb: int


def _tile_plan(seq, ctx_len):
    return _Tiles(tok=min(256, seq), ctx=min(256, ctx_len), chunk=min(256, ctx_len, seq),
                  route=LANES, peer=min(1024, seq), rpb=8)


def _params(sem, vmem=VMEM_LIMIT):
    return pltpu.CompilerParams(dimension_semantics=sem, vmem_limit_bytes=vmem)


def _gelu_tanh(x):
    return 0.5 * x * (1.0 + jnp.tanh(math.sqrt(2.0 / math.pi) * (x + 0.044715 * (x * x * x))))


def _sigmoid(x):
    return 1.0 / (1.0 + jnp.exp(-x))


def _mod_kernel(cb_ref, w_ref, b_ref, o_ref, a_scr, *, n_rows):
    d_model, tn = w_ref.shape
    reps = tn // LANES

    @pl.when(pl.program_id(0) == 0)
    def _():
        cb = cb_ref[...]
        a_scr[...] = cb * _sigmoid(cb)

    def body(kb, accs):
        k0 = pl.multiple_of(kb * 8, 8)
        wt = w_ref[pl.ds(k0, 8), :]
        return tuple(acc + jnp.concatenate([a_scr[r, pl.ds(k0, 8), :]] * reps, axis=1) * wt
                     for r, acc in enumerate(accs))

    accs = lax.fori_loop(0, d_model // 8, body,
                         tuple(jnp.zeros((8, tn), F32) for _ in range(n_rows)), unroll=2)
    for r in range(8):
        if r < n_rows:
            o_ref[r:r + 1, :] = jnp.sum(accs[r], axis=0, keepdims=True) + b_ref[...]
        else:
            o_ref[r:r + 1, :] = jnp.zeros((1, tn), F32)


def _mod_call(c, c_ctx, w_mod, b_mod):
    bsz, d_model = c.shape
    n_out = w_mod.shape[1]
    n_rows = bsz + 1
    assert n_rows <= 8
    cb = jnp.broadcast_to(jnp.concatenate([c, c_ctx[None]], axis=0)[:, :, None], (n_rows, d_model, LANES))
    tn = 512
    return pl.pallas_call(
        functools.partial(_mod_kernel, n_rows=n_rows),
        grid=(n_out // tn,),
        in_specs=[pl.BlockSpec((n_rows, d_model, LANES), lambda j: (0, 0, 0)),
                  pl.BlockSpec((d_model, tn), lambda j: (0, j)),
                  pl.BlockSpec((1, tn), lambda j: (0, j))],
        out_specs=pl.BlockSpec((8, tn), lambda j: (0, j)),
        out_shape=jax.ShapeDtypeStruct((8, n_out), F32),
        scratch_shapes=[pltpu.VMEM((n_rows, d_model, LANES), F32)],
        compiler_params=_params(("arbitrary",)),
        name="mod",
    )(cb, w_mod, b_mod[None])


def _modulated_norm(x, nw, shift, scale):
    ms = jnp.mean(x * x, axis=-1, keepdims=True)
    y = x * lax.rsqrt(ms + EPS) * nw
    return y * (1.0 + scale) + shift


def _inproj_kernel(x_ref, mod_ref, nw_ref, w_ref, *o_refs, d_model, widths, ctx_row):
    row = ctx_row if ctx_row is not None else pl.program_id(0)
    shift = mod_ref[pl.ds(row, 1), 0:d_model]
    scale = mod_ref[pl.ds(row, 1), d_model:2 * d_model]
    h = _modulated_norm(x_ref[0], nw_ref[...], shift, scale).astype(BF16)
    c0 = 0
    for o_ref, wd in zip(o_refs, widths):
        o_ref[0] = jnp.dot(h, w_ref[:, c0:c0 + wd], preferred_element_type=F32).astype(o_ref.dtype)
        c0 += wd


def _inproj_call(x, mod, norm_w, w_bf, widths, ctx_row, tm, name):
    bsz, seq, d_model = x.shape
    n_in = w_bf.shape[1]
    assert sum(widths) == n_in and seq % tm == 0
    kern = functools.partial(_inproj_kernel, d_model=d_model, widths=widths, ctx_row=ctx_row)
    return pl.pallas_call(
        kern,
        grid=(bsz, seq // tm),
        in_specs=[pl.BlockSpec((1, tm, d_model), lambda b, i: (b, i, 0)),
                  pl.BlockSpec(mod.shape, lambda b, i: (0, 0)),
                  pl.BlockSpec((1, d_model), lambda b, i: (0, 0)),
                  pl.BlockSpec((d_model, n_in), lambda b, i: (0, 0))],
        out_specs=[pl.BlockSpec((1, tm, wd), lambda b, i: (b, i, 0)) for wd in widths],
        out_shape=[jax.ShapeDtypeStruct((bsz, seq, wd), BF16) for wd in widths],
        compiler_params=_params(("arbitrary", "arbitrary")),
        name=name,
    )(x, mod, norm_w[None], w_bf)


def _disc_kernel(are_ref, aim_ref, ldt_ref, bre_ref, bim_ref,
                 abr_ref, abi_ref, bbr_ref, bbi_ref, bar_ref, bai_ref):
    a_re = are_ref[...]
    a_im = aim_ref[...]
    dt = jnp.exp(ldt_ref[...])
    mag = jnp.exp(dt * a_re)
    ab_re = mag * jnp.cos(dt * a_im)
    ab_im = mag * jnp.sin(dt * a_im)
    den = a_re * a_re + a_im * a_im
    nr = ab_re - 1.0
    ni = ab_im
    f_re = (nr * a_re + ni * a_im) / den
    f_im = (ni * a_re - nr * a_im) / den
    b_re = bre_ref[...]
    b_im = bim_ref[...]
    bb_re = f_re * b_re - f_im * b_im
    bb_im = f_re * b_im + f_im * b_re
    abr_ref[...] = ab_re
    abi_ref[...] = ab_im
    bbr_ref[...] = bb_re
    bbi_ref[...] = bb_im
    bar_ref[...] = ab_re * bb_re - ab_im * bb_im
    bai_ref[...] = ab_re * bb_im + ab_im * bb_re


def _disc_call(a_re, a_im, log_dt, b_re, b_im):
    nd, g, p, h = b_re.shape
    rows, cols = nd * g, p * h
    rep = lambda a: jnp.broadcast_to(a[..., None], (nd, g, p, h)).reshape(rows, cols)
    args = (rep(a_re), rep(a_im),
            jnp.broadcast_to(log_dt[..., None, None], (nd, g, p, h)).reshape(rows, cols),
            b_re.reshape(rows, cols), b_im.reshape(rows, cols))
    outs = pl.pallas_call(
        _disc_kernel,
        out_shape=[jax.ShapeDtypeStruct((rows, cols), F32)] * 6,
        name="s5_disc",
    )(*args)
    ab_re, ab_im, bb_re, bb_im, ba_re, ba_im = (o.reshape(nd, g, p, h) for o in outs)
    return ab_re[..., 0], ab_im[..., 0], bb_re, bb_im, ba_re, ba_im


def _s5_weights(ab_re, ab_im, bb_re, bb_im, ba_re, ba_im, c_re, c_im):
    nd, g, p, h = bb_re.shape
    gb = S5_GROUPS_PER_BLOCK
    nb = g // gb
    eye = jnp.eye(gb, dtype=bool)

    def in_mat(bb):
        t = bb.reshape(nd, nb, gb, p, h).transpose(0, 1, 2, 4, 3)
        t = jnp.where(eye[None, None, :, None, :, None], t[:, :, :, :, None, :], 0.0)
        return t.reshape(nd, nb, gb * h, gb * p)

    def out_mat(cc):
        t = cc.reshape(nd, nb, gb, h, p).transpose(0, 1, 2, 4, 3)
        t = jnp.where(eye[None, None, :, None, :, None], t[:, :, :, :, None, :], 0.0)
        return t.reshape(nd, nb, gb * p, gb * h)

    w_own = jnp.concatenate([in_mat(bb_re), in_mat(bb_im)], axis=-1)
    w_par = jnp.concatenate([in_mat(ba_re), in_mat(ba_im)], axis=-1)
    w_b = jnp.concatenate([w_own, w_par], axis=-2).astype(BF16)
    w_c = jnp.concatenate([out_mat(c_re.astype(F32)), -out_mat(c_im.astype(F32))], axis=-2).astype(BF16)
    a_r = ab_re.reshape(nd, nb, gb * p)
    a_i = ab_im.reshape(nd, nb, gb * p)
    return w_b, w_c, a_r, a_i


S5_ROWS = 4


def _s5_kernel(zc_ref, zl_ref, wb_ref, wc_ref, ar_ref, ai_ref, y_ref, sbuf0, sbuf1, h_ref,
               *, n_ctx_chunks, chunk, nb, half, nbat):
    d = pl.program_id(1)
    c = pl.program_id(2)
    sbufs = (sbuf0, sbuf1)
    cw = 2 * half
    bw = wb_ref.shape[2] // 2
    nl = cw // LANES
    hl = half // LANES
    nhb = nb // S5_ROWS
    npair = chunk // 2

    @pl.when(c == 0)
    def _():
        h_ref[...] = jnp.zeros_like(h_ref)

    sub8 = lax.broadcasted_iota(I32, (8, LANES), 0)
    second = (sub8 >= S5_ROWS) == (d == 0)
    row_par = lax.broadcasted_iota(I32, (chunk, 1), 0) % 2
    is_second = row_par == (1 - d)

    def partner_rows(x):
        xf = x.astype(F32)
        partner = jnp.where(d == 0, pltpu.roll(xf, 1, 0), pltpu.roll(xf, chunk - 1, 0))
        return jnp.where(is_second, partner, 0.0).astype(BF16)

    def project_block(e, x, xp, k):
        hb, kk = divmod(k, S5_ROWS)
        lhs = jnp.concatenate([x[:, k * bw:(k + 1) * bw], xp[:, k * bw:(k + 1) * bw]], axis=1)
        bu = jnp.dot(lhs, wb_ref[0, k], preferred_element_type=F32)
        for l in range(nl):
            sbufs[e][hb * nl + l, pl.ds(kk, chunk, stride=S5_ROWS), :] = bu[:, l * LANES:(l + 1) * LANES]

    p_r, p_i = [], []
    for hb in range(nhb):
        for l in range(hl):
            a_r = ar_ref[0, hb * S5_ROWS:(hb + 1) * S5_ROWS, l * LANES:(l + 1) * LANES]
            a_i = ai_ref[0, hb * S5_ROWS:(hb + 1) * S5_ROWS, l * LANES:(l + 1) * LANES]
            a_r = jnp.concatenate([a_r, a_r], axis=0)
            a_i = jnp.concatenate([a_i, a_i], axis=0)
            p_r.append(jnp.where(second, a_r * a_r - a_i * a_i, a_r))
            p_i.append(jnp.where(second, 2.0 * a_r * a_i, a_i))
    nch = nhb * hl

    def scan_steps(e, carry, s0, s1):
        for s in range(s0, s1):
            m = s + d * (npair - 1 - 2 * s)
            row = pl.multiple_of(m * 8, 8)
            out = []
            for hb in range(nhb):
                for l in range(hl):
                    ch = hb * hl + l
                    c_r, c_i = carry[2 * ch], carry[2 * ch + 1]
                    pr, pi = hb * nl + l, hb * nl + hl + l
                    n_r = p_r[ch] * c_r - p_i[ch] * c_i + sbufs[e][pr, pl.ds(row, 8), :]
                    n_i = p_r[ch] * c_i + p_i[ch] * c_r + sbufs[e][pi, pl.ds(row, 8), :]
                    sbufs[e][pr, pl.ds(row, 8), :] = n_r
                    sbufs[e][pi, pl.ds(row, 8), :] = n_i
                    out += [jnp.where(second, n_r, pltpu.roll(n_r, S5_ROWS, 0)),
                            jnp.where(second, n_i, pltpu.roll(n_i, S5_ROWS, 0))]
            carry = out
        return carry

    def load_carry(e):
        return [h_ref[e, i] for i in range(2 * nch)]

    def store_carry(e, carry):
        for i in range(2 * nch):
            h_ref[e, i] = carry[i]

    def readout_block(e, k):
        hb, kk = divmod(k, S5_ROWS)
        hk = jnp.concatenate([sbufs[e][hb * nl + l, pl.ds(kk, chunk, stride=S5_ROWS), :]
                              for l in range(nl)], axis=1)
        y_ref[0, e, :, k * bw:(k + 1) * bw] = jnp.dot(hk.astype(BF16), wc_ref[0, k],
                                                      preferred_element_type=F32)

    def run(src_ref, with_readout):
        assert nbat == 2 and npair % nb == 0
        per = npair // nb
        xs = [src_ref[e] for e in range(nbat)]
        xps = [partner_rows(x) for x in xs]
        for k in range(nb):
            project_block(0, xs[0], xps[0], k)
        carry = load_carry(0)
        for k in range(nb):
            project_block(1, xs[1], xps[1], k)
            carry = scan_steps(0, carry, k * per, (k + 1) * per)
        store_carry(0, carry)
        carry = load_carry(1)
        for k in range(nb):
            if with_readout:
                readout_block(0, k)
            carry = scan_steps(1, carry, k * per, (k + 1) * per)
        store_carry(1, carry)
        if with_readout:
            for k in range(nb):
                readout_block(1, k)

    @pl.when(c < n_ctx_chunks)
    def _():
        run(zc_ref, False)

    @pl.when(c >= n_ctx_chunks)
    def _():
        run(zl_ref, True)


def _s5_call(zc, zl, w_b, w_c, a_r, a_i, chunk):
    bsz, lc, width = zc.shape
    seq = zl.shape[1]
    nd, nb, bw2, cw = w_b.shape
    bw = bw2 // 2
    half = cw // 2
    assert nb % S5_ROWS == 0 and lc % chunk == 0 and seq % chunk == 0 and width == nb * bw and chunk % 8 == 0
    ncc, nlc = lc // chunk, seq // chunk
    assert bsz % 2 == 0
    nbat = 2
    nplanes = (nb // S5_ROWS) * (cw // LANES)

    def ctx_idx(b, d, c):
        cc = jnp.minimum(c, ncc - 1)
        return (b, cc + d * (ncc - 1 - 2 * cc), 0)

    def lat_chunk(d, c):
        lc_ = jnp.maximum(c - ncc, 0)
        return lc_ + d * (nlc - 1 - 2 * lc_)

    kern = functools.partial(_s5_kernel, n_ctx_chunks=ncc, chunk=chunk, nb=nb, half=half, nbat=nbat)
    return pl.pallas_call(
        kern,
        grid=(bsz // nbat, nd, ncc + nlc),
        in_specs=[pl.BlockSpec((nbat, chunk, width), ctx_idx),
                  pl.BlockSpec((nbat, chunk, width), lambda b, d, c: (b, lat_chunk(d, c), 0)),
                  pl.BlockSpec((1, nb, bw2, cw), lambda b, d, c: (d, 0, 0, 0)),
                  pl.BlockSpec((1, nb, cw, bw), lambda b, d, c: (d, 0, 0, 0)),
                  pl.BlockSpec((1, nb, half), lambda b, d, c: (d, 0, 0)),
                  pl.BlockSpec((1, nb, half), lambda b, d, c: (d, 0, 0))],
        out_specs=pl.BlockSpec((1, nbat, chunk, width), lambda b, d, c: (d, b, lat_chunk(d, c), 0)),
        out_shape=jax.ShapeDtypeStruct((nd, bsz, seq, width), F32),
        scratch_shapes=[pltpu.VMEM((nplanes, chunk * S5_ROWS, LANES), F32),
                        pltpu.VMEM((nplanes, chunk * S5_ROWS, LANES), F32),
                        pltpu.VMEM((nbat, nplanes, 8, LANES), F32)],
        compiler_params=_params(("arbitrary", "arbitrary", "arbitrary")),
        name="s5_scan",
    )(zc, zl, w_b, w_c, a_r, a_i)


def _mix_kernel(x_ref, yf_ref, yb_ref, us_ref, up_ref, gt_ref, mod_ref, dsk_ref, wglu_ref, bglu_ref,
                pw_ref, psc_ref, pa_ref, pb_ref, wo_ref, nw2_ref, wq_ref,
                x1_ref, h2_ref, q_ref, *, d_model, tm):
    b = pl.program_id(0)
    gate1 = mod_ref[pl.ds(b, 1), 2 * d_model:3 * d_model]
    shift2 = mod_ref[pl.ds(b, 1), 3 * d_model:4 * d_model]
    scale2 = mod_ref[pl.ds(b, 1), 4 * d_model:5 * d_model]

    u_s = us_ref[0].astype(F32)
    y = yf_ref[0, 0] + yb_ref[0, 0] + dsk_ref[...] * u_s
    ge = _gelu_tanh(y)
    lin = jnp.dot(ge.astype(BF16), wglu_ref[...], preferred_element_type=F32) + bglu_ref[...]
    y_a = (ge * _sigmoid(lin)).astype(BF16)

    u_p = up_ref[0]
    ri = lax.broadcasted_iota(I32, (tm, tm), 0)
    ci = lax.broadcasted_iota(I32, (tm, tm), 1)
    same_row = (ri // GRID_W) == (ci // GRID_W)
    pos_r = ri % GRID_W
    pos_c = ci % GRID_W
    pos1 = lax.broadcasted_iota(I32, (tm, 1), 0) % GRID_W
    gw = pw_ref.shape[1]
    y_b_parts = []
    for gi, w in enumerate(POOL_WINDOWS):
        lo = jnp.maximum(pos_r - w // 2, 0)
        hi = jnp.minimum(pos_r + w // 2 - 1, GRID_W - 1)
        band = jnp.where(same_row & (pos_c >= lo) & (pos_c <= hi), 1.0, 0.0).astype(BF16)
        ug = u_p[:, gi * gw:(gi + 1) * gw]
        win = jnp.dot(band, ug, preferred_element_type=F32)
        cnt = (jnp.minimum(pos1 + w // 2 - 1, GRID_W - 1) - jnp.maximum(pos1 - w // 2, 0) + 1).astype(F32)
        p = win / cnt - ug.astype(F32)
        y_b_parts.append(jnp.dot(p.astype(BF16), pw_ref[gi], preferred_element_type=F32))
    y_b = (jnp.concatenate(y_b_parts, axis=-1) * psc_ref[...]).astype(BF16)

    g_a = _sigmoid(gt_ref[0, :, 0:d_model].astype(F32))
    g_b = _sigmoid(gt_ref[0, :, d_model:2 * d_model].astype(F32))
    m = (g_a * jnp.dot(y_a, pa_ref[...], preferred_element_type=F32)
         + g_b * jnp.dot(y_b, pb_ref[...], preferred_element_type=F32))
    x1 = x_ref[0] + gate1 * jnp.dot(m.astype(BF16), wo_ref[...], preferred_element_type=F32)
    x1_ref[0] = x1

    h2 = _modulated_norm(x1, nw2_ref[...], shift2, scale2).astype(BF16)
    h2_ref[0] = h2
    q_ref[0] = jnp.dot(h2, wq_ref[...], preferred_element_type=F32)


def _mix_call(x, y2, us, up, gates, mod, d_skip, w_glu, b_glu, pool_w, pool_scale,
              proj_a, proj_b, w_out, norm2_w, w_q, tm):
    bsz, seq, d_model = x.shape
    sw = us.shape[2]
    pwid = up.shape[2]
    assert seq % tm == 0 and tm % GRID_W == 0
    full = lambda a: pl.BlockSpec(a.shape, lambda b, i: (0,) * a.ndim)
    row = lambda a: a[None]
    tok = lambda wd: pl.BlockSpec((1, tm, wd), lambda b, i: (b, i, 0))
    args = (x, y2, y2, us, up, gates, mod, row(d_skip), w_glu.astype(BF16), row(b_glu),
            pool_w.astype(BF16), row(pool_scale), proj_a.astype(BF16), proj_b.astype(BF16),
            w_out.astype(BF16), row(norm2_w), w_q.astype(BF16))
    in_specs = [tok(d_model),
                pl.BlockSpec((1, 1, tm, sw), lambda b, i: (0, b, i, 0)),
                pl.BlockSpec((1, 1, tm, sw), lambda b, i: (1, b, i, 0)),
                tok(sw), tok(pwid), tok(2 * d_model)] + [full(a) for a in args[6:]]
    return pl.pallas_call(
        functools.partial(_mix_kernel, d_model=d_model, tm=tm),
        grid=(bsz, seq // tm),
        in_specs=in_specs,
        out_specs=[tok(d_model), tok(d_model), tok(d_model)],
        out_shape=[jax.ShapeDtypeStruct((bsz, seq, d_model), F32),
                   jax.ShapeDtypeStruct((bsz, seq, d_model), BF16),
                   jax.ShapeDtypeStruct((bsz, seq, d_model), F32)],
        compiler_params=_params(("arbitrary", "arbitrary")),
        name="mix",
    )(*args)


def _oddeven_merge_sort_pairs(n):
    pairs = []

    def merge(lo, hi, r):
        step = r * 2
        if step < hi - lo:
            merge(lo, hi, step)
            merge(lo + r, hi, step)
            pairs.extend((i, i + r) for i in range(lo + r, hi - r, step))
        else:
            pairs.append((lo, lo + r))

    def sort(lo, hi):
        if hi > lo:
            mid = lo + (hi - lo) // 2
            sort(lo, mid)
            sort(mid + 1, hi)
            merge(lo, hi, 1)

    sort(0, n - 1)
    return pairs


_SORT16_PAIRS = _oddeven_merge_sort_pairs(PEER_KEYS // 8)


def _route_kernel(q_ref, k1_ref, k2_ref, o_ref,
                  v1s, x1s, v2s, x2s, es, gws, tes, tgs, *, tq, rpb):
    neg = -jnp.inf
    nk = PEER_KEYS
    ns = PEER_HEADS * PEER_TOPK
    sub8 = lax.broadcasted_iota(I32, (8, tq), 0)
    lt2, lt4, lt5, lt6 = sub8 < 2, sub8 < 4, sub8 < 5, sub8 < 6
    s8 = sub8.astype(F32)
    cio = jnp.concatenate([s8, s8 + 8.0, s8 + 16.0,
                           jnp.where(lt5, s8 + 32.0, s8 + 59.0),
                           jnp.where(lt4, s8 + 48.0, jnp.where(lt6, s8 + 76.0, s8 + 90.0)),
                           jnp.where(lt2, s8 + 112.0, s8 * 16.0 + 96.0),
                           s8 * 16.0 + 224.0], axis=0)
    big = 1024.0
    sub_s = lax.broadcasted_iota(I32, (nk, ns), 0).astype(F32)
    hd = q_ref.shape[1] // PEER_HEADS
    half = hd // 2
    tokens_per_head = tq // PEER_HEADS

    @pl.when(pl.program_id(0) == 0)
    def _():
        tes[...] = jnp.zeros_like(tes)
        tgs[...] = jnp.zeros_like(tgs)

    def top16(s, vals_ref, idx_ref):
        val = [s[8 * v:8 * v + 8, :] for v in range(nk // 8)]
        idx = [s8 + float(8 * v) for v in range(nk // 8)]
        for (a, b) in _SORT16_PAIRS:
            va, vb, ia, ib = val[a], val[b], idx[a], idx[b]
            first = (va > vb) | ((va == vb) & (ia < ib))
            val[a], val[b] = jnp.maximum(va, vb), jnp.minimum(va, vb)
            idx[a], idx[b] = jnp.where(first, ia, ib), jnp.where(first, ib, ia)
        for k in range(PEER_TOPK):
            m = jnp.max(val[0], axis=0, keepdims=True)
            i = jnp.min(jnp.where(val[0] == m, idx[0], big), axis=0, keepdims=True)
            vals_ref[k:k + 1, :] = m
            idx_ref[k:k + 1, :] = i
            win = idx[0] == i
            for d in range(PEER_TOPK - 1 - k):
                val[d] = jnp.where(win, val[d + 1], val[d])
                idx[d] = jnp.where(win, idx[d + 1], idx[d])

    def route_head(h):
        q1 = q_ref[:, h * hd:h * hd + half]
        q2 = q_ref[:, h * hd + half:(h + 1) * hd]
        nt = (((1,), (1,)), ((), ()))
        s1 = lax.dot_general(k1_ref[...], q1, nt, precision=lax.Precision.HIGHEST,
                             preferred_element_type=F32)
        s2 = lax.dot_general(k2_ref[...], q2, nt, precision=lax.Precision.HIGHEST,
                             preferred_element_type=F32)
        top16(s1, v1s, x1s)
        top16(s2, v2s, x2s)

        v2a, v2b = v2s[0:8, :], v2s[8:16, :]
        x2a, x2b = x2s[0:8, :], x2s[8:16, :]
        r1v = pltpu.roll(v1s[8:16, :], 2, 0)
        r1x = pltpu.roll(x1s[8:16, :], 2, 0)
        v1r = lambda p: v1s[p:p + 1, :]
        x1r = lambda p: x1s[p:p + 1, :]
        pieces = [
            (v1r(0) + v2a, x1r(0), x2a),
            (v1r(0) + v2b, x1r(0), x2b),
            (v1r(1) + v2a, x1r(1), x2a),
            (jnp.where(lt5, v1r(2), v1r(4)) + jnp.where(lt5, v2a, pltpu.roll(v2a, 5, 0)),
             jnp.where(lt5, x1r(2), x1r(4)), jnp.where(lt5, x2a, pltpu.roll(x2a, 5, 0))),
            (jnp.where(lt4, v1r(3), jnp.where(lt6, v1r(5), v1r(6)))
             + jnp.where(lt4, v2a, jnp.where(lt6, pltpu.roll(v2a, 4, 0), pltpu.roll(v2a, 6, 0))),
             jnp.where(lt4, x1r(3), jnp.where(lt6, x1r(5), x1r(6))),
             jnp.where(lt4, x2a, jnp.where(lt6, pltpu.roll(x2a, 4, 0), pltpu.roll(x2a, 6, 0)))),
            (jnp.where(lt2, v1r(7), r1v) + jnp.where(lt2, v2a, v2s[0:1, :]),
             jnp.where(lt2, x1r(7), r1x), jnp.where(lt2, x2a, x2s[0:1, :])),
            (jnp.where(lt2, r1v + v2s[0:1, :], neg), r1x, jnp.broadcast_to(x2s[0:1, :], (8, tq))),
        ]
        cand = jnp.concatenate([pc[0] for pc in pieces], axis=0)
        ceid = jnp.concatenate([jnp.broadcast_to(pc[1], (8, tq)) * float(nk) + pc[2] for pc in pieces], axis=0)

        tops = []
        for k in range(PEER_TOPK):
            m = jnp.max(cand, axis=0, keepdims=True)
            i = jnp.min(jnp.where(cand == m, cio, big), axis=0, keepdims=True)
            hit = cio == i
            r = h * PEER_TOPK + k
            es[r:r + 1, :] = jnp.sum(jnp.where(hit, ceid, 0.0), axis=0, keepdims=True)
            tops.append(m)
            cand = jnp.where(hit, neg, cand)
        ex = [jnp.exp(t - tops[0]) for t in tops]
        tot = ex[0]
        for e in ex[1:]:
            tot = tot + e
        for k in range(PEER_TOPK):
            r = h * PEER_TOPK + k
            gws[r:r + 1, :] = ex[k] / tot

    def expand_tokens(t0):
        eid_8 = tes[t0:t0 + 8, :]
        i1_8 = jnp.floor(eid_8 * (1.0 / nk))
        i2_8 = eid_8 - i1_8 * float(nk)
        gw_8 = tgs[t0:t0 + 8, :]
        gh_8 = gw_8.astype(BF16).astype(F32)
        gl_8 = gw_8 - gh_8
        for u in range(8):
            hit1 = sub_s == i1_8[u:u + 1, :]
            a2 = jnp.concatenate([jnp.where(hit1, gh_8[u:u + 1, :], 0.0),
                                  jnp.where(hit1, gl_8[u:u + 1, :], 0.0)], axis=1).astype(BF16)
            b1 = jnp.where(sub_s == i2_8[u:u + 1, :], 1.0, 0.0)
            b2 = jnp.concatenate([b1, b1], axis=1).astype(BF16)
            gt = lax.dot_general(a2, b2, (((1,), (1,)), ((), ())), preferred_element_type=F32)
            row0 = (t0 + u) * rpb
            for jb in range(nk // rpb):
                o_ref[0, jb, row0:row0 + rpb, :] = gt[jb * rpb:(jb + 1) * rpb, :]

    for h in range(PEER_HEADS):
        route_head(h)
        for t0 in range(h * tokens_per_head, (h + 1) * tokens_per_head, 8):
            expand_tokens(t0)

    tes[...] = es[...].T
    tgs[...] = gws[...].T


def _route_call(q2d, k1, k2, tq, tm, rpb):
    n_tok, d_q = q2d.shape
    nk = PEER_KEYS
    ns = PEER_HEADS * PEER_TOPK
    assert n_tok % tm == 0 and tm % tq == 0 and ns == nk and tq == LANES and nk % rpb == 0
    n_tiles = n_tok // tq
    per = tm // tq

    def out_idx(i):
        p = jnp.maximum(i - 1, 0)
        return (p // per, 0, p % per, 0)

    return pl.pallas_call(
        functools.partial(_route_kernel, tq=tq, rpb=rpb),
        grid=(n_tiles + 1,),
        in_specs=[pl.BlockSpec((tq, d_q), lambda i: (jnp.minimum(i, n_tiles - 1), 0)),
                  pl.BlockSpec(k1.shape, lambda i: (0, 0)),
                  pl.BlockSpec(k2.shape, lambda i: (0, 0))],
        out_specs=pl.BlockSpec((1, nk // rpb, tq * rpb, nk), out_idx),
        out_shape=jax.ShapeDtypeStruct((n_tok // tm, nk // rpb, tm * rpb, nk), F32),
        scratch_shapes=[pltpu.VMEM((PEER_TOPK, tq), F32), pltpu.VMEM((PEER_TOPK, tq), F32),
                        pltpu.VMEM((PEER_TOPK, tq), F32), pltpu.VMEM((PEER_TOPK, tq), F32),
                        pltpu.VMEM((ns, tq), F32), pltpu.VMEM((ns, tq), F32),
                        pltpu.VMEM((tq, ns), F32), pltpu.VMEM((tq, ns), F32)],
        compiler_params=_params(("arbitrary",)),
        name="peer_route",
    )(q2d, k1, k2)


def _peer_kernel(h_ref, u_ref, v_ref, g_ref, x1_ref, mod_ref, fw_ref, o_ref,
                 *, d_model, rpb, tokens_per_batch, tm):
    i = pl.program_id(0)
    j = pl.program_id(1)

    @pl.when(j == 0)
    def _():
        o_ref[...] = jnp.zeros_like(o_ref)

    z = lax.dot_general(h_ref[...], u_ref[...], (((1,), (1,)), ((), ())), preferred_element_type=F32)
    gmat = jnp.concatenate([g_ref[0, 0, pl.ds(r, tm, stride=rpb), :] for r in range(rpb)], axis=1)
    w = (_gelu_tanh(z) * gmat).astype(BF16)
    o_ref[...] += jnp.dot(w, v_ref[...], preferred_element_type=F32)

    @pl.when(j == pl.num_programs(1) - 1)
    def _():
        b = (i * tm) // tokens_per_batch
        gate2 = mod_ref[pl.ds(b, 1), 5 * d_model:6 * d_model]
        x2 = x1_ref[...] + gate2 * o_ref[...]
        ms = jnp.mean(x2 * x2, axis=-1, keepdims=True)
        o_ref[...] = x2 * lax.rsqrt(ms + EPS) * fw_ref[...]


def _peer_call(h2, u_bf, v_bf, gmat, x1, mod, final_w, tokens_per_batch, tm, rpb):
    n_tok, d_model = h2.shape
    n_exp = v_bf.shape[0]
    nk = PEER_KEYS
    eb = rpb * nk
    assert n_tok % tm == 0 and n_exp % eb == 0 and tokens_per_batch % tm == 0
    assert gmat.shape == (n_tok // tm, n_exp // eb, tm * rpb, nk)
    kern = functools.partial(_peer_kernel, d_model=d_model, rpb=rpb,
                             tokens_per_batch=tokens_per_batch, tm=tm)
    once = pl.Buffered(1)
    return pl.pallas_call(
        kern,
        grid=(n_tok // tm, n_exp // eb),
        in_specs=[pl.BlockSpec((tm, d_model), lambda i, j: (i, 0), pipeline_mode=once),
                  pl.BlockSpec((eb, d_model), lambda i, j: (j, 0)),
                  pl.BlockSpec((eb, d_model), lambda i, j: (j, 0)),
                  pl.BlockSpec((1, 1, tm * rpb, nk), lambda i, j: (i, j, 0, 0)),
                  pl.BlockSpec((tm, d_model), lambda i, j: (i, 0), pipeline_mode=once),
                  pl.BlockSpec(mod.shape, lambda i, j: (0, 0)),
                  pl.BlockSpec((1, d_model), lambda i, j: (0, 0))],
        out_specs=pl.BlockSpec((tm, d_model), lambda i, j: (i, 0), pipeline_mode=once),
        out_shape=jax.ShapeDtypeStruct((n_tok, d_model), F32),
        compiler_params=_params(("arbitrary", "arbitrary"), PEER_VMEM_LIMIT),
        name="peer_experts",
    )(h2, u_bf, v_bf, gmat, x1, mod, final_w[None])


def kernel(x, c, ctx, c_ctx, w_mod, b_mod, norm1_w, norm2_w, w_in, s5_a_re, s5_a_im, s5_log_dt,
           s5_b_re, s5_b_im, s5_c_re, s5_c_im, s5_d, w_glu, b_glu, pool_w, pool_scale, proj_a,
           proj_b, w_out, peer_wq, peer_k1, peer_k2, peer_u, peer_v, final_w):
    bsz, seq, d_model = x.shape
    depth = w_mod.shape[0]
    assert depth == 1, "only the single-layer configuration is implemented"
    l = 0
    s5_width = s5_d.shape[1]
    pool_width = pool_scale.shape[1]

    mod = _mod_call(c, c_ctx, w_mod[l], b_mod[l])

    tiles = _tile_plan(seq, ctx.shape[1])
    w_in_bf = w_in[l].astype(BF16)
    us, up, gates = _inproj_call(x, mod, norm1_w[l], w_in_bf,
                                 (s5_width, pool_width, 2 * d_model), None, tiles.tok, "inproj_lat")
    (zc,) = _inproj_call(ctx, mod, norm1_w[l], w_in_bf[:, :s5_width], (s5_width,), bsz,
                         tiles.ctx, "inproj_ctx")

    disc = _disc_call(s5_a_re[l], s5_a_im[l], s5_log_dt[l], s5_b_re[l], s5_b_im[l])
    w_b, w_c, a_r, a_i = _s5_weights(*disc, s5_c_re[l], s5_c_im[l])
    y2 = _s5_call(zc, us, w_b, w_c, a_r, a_i, tiles.chunk)

    x1, h2, q = _mix_call(x, y2, us, up, gates, mod, s5_d[l], w_glu[l], b_glu[l], pool_w[l],
                          pool_scale[l], proj_a[l], proj_b[l], w_out[l], norm2_w[l], peer_wq[l], tiles.tok)

    n_tok = bsz * seq
    gmat = _route_call(q.reshape(n_tok, d_model), peer_k1[l], peer_k2[l], tiles.route, tiles.peer, tiles.rpb)
    out = _peer_call(h2.reshape(n_tok, d_model), peer_u[l].astype(BF16), peer_v[l].astype(BF16),
                     gmat, x1.reshape(n_tok, d_model), mod, final_w, seq, tiles.peer, tiles.rpb)
    return out.reshape(bsz, seq, d_model)
```

```python
import functools
import math
from typing import NamedTuple

import jax
import jax.numpy as jnp
from jax import lax
from jax.experimental import pallas as pl
from jax.experimental.pallas import tpu as pltpu

F32 = jnp.float32
BF16 = jnp.bfloat16
I32 = jnp.int32

LANES = 128
EPS = 1e-6
GRID_W = 64
S5_GROUPS_PER_BLOCK = 8
POOL_WINDOWS = (2, 4, 8, 16)
PEER_HEADS = 8
PEER_KEYS = 128
PEER_TOPK = 16

V7X_VMEM_BYTES = 64 * 1024 * 1024
VMEM_LIMIT = V7X_VMEM_BYTES - 8 * 1024 * 1024
WIDE_VMEM_LIMIT = V7X_VMEM_BYTES - 4 * 1024 * 1024


class _Tiles(NamedTuple):
    proj: int
    tok: int
    ctx: int
    chunk: int
    route: int
    peer: int
    rpb: int


def _tile_plan(seq, ctx_len):
    return _Tiles(proj=min(512, seq), tok=min(256, seq), ctx=min(256, ctx_len), chunk=min(256, ctx_len, seq),
                  route=LANES, peer=min(1024, seq), rpb=8)


def _params(sem, vmem=VMEM_LIMIT):
    return pltpu.CompilerParams(dimension_semantics=sem, vmem_limit_bytes=vmem)


def _gelu_tanh(x):
    return 0.5 * x * (1.0 + jnp.tanh(math.sqrt(2.0 / math.pi) * (x + 0.044715 * (x * x * x))))


def _sigmoid(x):
    return 1.0 / (1.0 + jnp.exp(-x))


def _mod_kernel(cb_ref, w_ref, b_ref, o_ref, a_scr, *, n_rows):
    d_model, tn = w_ref.shape
    reps = tn // LANES

    @pl.when(pl.program_id(0) == 0)
    def _():
        cb = cb_ref[...]
        a_scr[...] = cb * _sigmoid(cb)

    def body(kb, accs):
        k0 = pl.multiple_of(kb * 8, 8)
        wt = w_ref[pl.ds(k0, 8), :]
        return tuple(acc + jnp.concatenate([a_scr[r, pl.ds(k0, 8), :]] * reps, axis=1) * wt
                     for r, acc in enumerate(accs))

    accs = lax.fori_loop(0, d_model // 8, body,
                         tuple(jnp.zeros((8, tn), F32) for _ in range(n_rows)), unroll=2)
    for r in range(8):
        if r < n_rows:
            o_ref[r:r + 1, :] = jnp.sum(accs[r], axis=0, keepdims=True) + b_ref[...]
        else:
            o_ref[r:r + 1, :] = jnp.zeros((1, tn), F32)


def _mod_call(c, c_ctx, w_mod, b_mod):
    bsz, d_model = c.shape
    n_out = w_mod.shape[1]
    n_rows = bsz + 1
    assert n_rows <= 8
    cb = jnp.broadcast_to(jnp.concatenate([c, c_ctx[None]], axis=0)[:, :, None], (n_rows, d_model, LANES))
    tn = 512
    return pl.pallas_call(
        functools.partial(_mod_kernel, n_rows=n_rows),
        grid=(n_out // tn,),
        in_specs=[pl.BlockSpec((n_rows, d_model, LANES), lambda j: (0, 0, 0)),
                  pl.BlockSpec((d_model, tn), lambda j: (0, j)),
                  pl.BlockSpec((1, tn), lambda j: (0, j))],
        out_specs=pl.BlockSpec((8, tn), lambda j: (0, j)),
        out_shape=jax.ShapeDtypeStruct((8, n_out), F32),
        scratch_shapes=[pltpu.VMEM((n_rows, d_model, LANES), F32)],
        compiler_params=_params(("arbitrary",)),
        name="mod",
    )(cb, w_mod, b_mod[None])


def _modulated_norm(x, nw, shift, scale):
    ms = jnp.mean(x * x, axis=-1, keepdims=True)
    y = x * lax.rsqrt(ms + EPS) * nw
    return y * (1.0 + scale) + shift


def _inproj_kernel(x_ref, mod_ref, nw_ref, w_ref, *o_refs, d_model, widths, ctx_row):
    row = ctx_row if ctx_row is not None else pl.program_id(0)
    shift = mod_ref[pl.ds(row, 1), 0:d_model]
    scale = mod_ref[pl.ds(row, 1), d_model:2 * d_model]
    h = _modulated_norm(x_ref[0], nw_ref[...], shift, scale).astype(BF16)
    c0 = 0
    for o_ref, wd in zip(o_refs, widths):
        o_ref[0] = jnp.dot(h, w_ref[:, c0:c0 + wd], preferred_element_type=F32).astype(o_ref.dtype)
        c0 += wd


def _inproj_call(x, mod, norm_w, w_bf, widths, ctx_row, tm, name):
    bsz, seq, d_model = x.shape
    n_in = w_bf.shape[1]
    assert sum(widths) == n_in and seq % tm == 0
    kern = functools.partial(_inproj_kernel, d_model=d_model, widths=widths, ctx_row=ctx_row)
    return pl.pallas_call(
        kern,
        grid=(bsz, seq // tm),
        in_specs=[pl.BlockSpec((1, tm, d_model), lambda b, i: (b, i, 0)),
                  pl.BlockSpec(mod.shape, lambda b, i: (0, 0)),
                  pl.BlockSpec((1, d_model), lambda b, i: (0, 0)),
                  pl.BlockSpec((d_model, n_in), lambda b, i: (0, 0))],
        out_specs=[pl.BlockSpec((1, tm, wd), lambda b, i: (b, i, 0)) for wd in widths],
        out_shape=[jax.ShapeDtypeStruct((bsz, seq, wd), BF16) for wd in widths],
        compiler_params=_params(("arbitrary", "arbitrary"), WIDE_VMEM_LIMIT),
        name=name,
    )(x, mod, norm_w[None], w_bf)


def _disc_kernel(are_ref, aim_ref, ldt_ref, bre_ref, bim_ref,
                 abr_ref, abi_ref, bbr_ref, bbi_ref, bar_ref, bai_ref):
    a_re = are_ref[...]
    a_im = aim_ref[...]
    dt = jnp.exp(ldt_ref[...])
    mag = jnp.exp(dt * a_re)
    ab_re = mag * jnp.cos(dt * a_im)
    ab_im = mag * jnp.sin(dt * a_im)
    den = a_re * a_re + a_im * a_im
    nr = ab_re - 1.0
    ni = ab_im
    f_re = (nr * a_re + ni * a_im) / den
    f_im = (ni * a_re - nr * a_im) / den
    b_re = bre_ref[...]
    b_im = bim_ref[...]
    bb_re = f_re * b_re - f_im * b_im
    bb_im = f_re * b_im + f_im * b_re
    abr_ref[...] = ab_re
    abi_ref[...] = ab_im
    bbr_ref[...] = bb_re
    bbi_ref[...] = bb_im
    bar_ref[...] = ab_re * bb_re - ab_im * bb_im
    bai_ref[...] = ab_re * bb_im + ab_im * bb_re


def _disc_call(a_re, a_im, log_dt, b_re, b_im):
    nd, g, p, h = b_re.shape
    rows, cols = nd * g, p * h
    rep = lambda a: jnp.broadcast_to(a[..., None], (nd, g, p, h)).reshape(rows, cols)
    args = (rep(a_re), rep(a_im),
            jnp.broadcast_to(log_dt[..., None, None], (nd, g, p, h)).reshape(rows, cols),
            b_re.reshape(rows, cols), b_im.reshape(rows, cols))
    outs = pl.pallas_call(
        _disc_kernel,
        out_shape=[jax.ShapeDtypeStruct((rows, cols), F32)] * 6,
        name="s5_disc",
    )(*args)
    ab_re, ab_im, bb_re, bb_im, ba_re, ba_im = (o.reshape(nd, g, p, h) for o in outs)
    return ab_re[..., 0], ab_im[..., 0], bb_re, bb_im, ba_re, ba_im


def _s5_weights(ab_re, ab_im, bb_re, bb_im, ba_re, ba_im, c_re, c_im):
    nd, g, p, h = bb_re.shape
    gb = S5_GROUPS_PER_BLOCK
    nb = g // gb
    eye = jnp.eye(gb, dtype=bool)

    def in_mat(bb):
        t = bb.reshape(nd, nb, gb, p, h).transpose(0, 1, 2, 4, 3)
        t = jnp.where(eye[None, None, :, None, :, None], t[:, :, :, :, None, :], 0.0)
        return t.reshape(nd, nb, gb * h, gb * p)

    def out_mat(cc):
        t = cc.reshape(nd, nb, gb, h, p).transpose(0, 1, 2, 4, 3)
        t = jnp.where(eye[None, None, :, None, :, None], t[:, :, :, :, None, :], 0.0)
        return t.reshape(nd, nb, gb * p, gb * h)

    w_own = jnp.concatenate([in_mat(bb_re), in_mat(bb_im)], axis=-1)
    w_par = jnp.concatenate([in_mat(ba_re), in_mat(ba_im)], axis=-1)
    w_b = jnp.concatenate([w_own, w_par], axis=-2).astype(BF16)
    w_c = jnp.concatenate([out_mat(c_re.astype(F32)), -out_mat(c_im.astype(F32))], axis=-2).astype(BF16)
    a_r = ab_re.reshape(nd, nb, gb * p)
    a_i = ab_im.reshape(nd, nb, gb * p)
    return w_b, w_c, a_r, a_i


S5_ROWS = 4


def _s5_kernel(zc_ref, zl_ref, wb_ref, wc_ref, ar_ref, ai_ref, y_ref, sbuf0, sbuf1, h_ref,
               *, n_ctx_chunks, chunk, nb, half, nbat):
    d = pl.program_id(1)
    c = pl.program_id(2)
    sbufs = (sbuf0, sbuf1)
    cw = 2 * half
    bw = wb_ref.shape[2] // 2
    nl = cw // LANES
    hl = half // LANES
    nhb = nb // S5_ROWS
    npair = chunk // 2

    @pl.when(c == 0)
    def _():
        h_ref[...] = jnp.zeros_like(h_ref)

    sub8 = lax.broadcasted_iota(I32, (8, LANES), 0)
    second = (sub8 >= S5_ROWS) == (d == 0)
    row_par = lax.broadcasted_iota(I32, (chunk, 1), 0) % 2
    is_second = row_par == (1 - d)

    def partner_rows(x):
        xf = x.astype(F32)
        partner = jnp.where(d == 0, pltpu.roll(xf, 1, 0), pltpu.roll(xf, chunk - 1, 0))
        return jnp.where(is_second, partner, 0.0).astype(BF16)

    def project_block(e, x, xp, k):
        hb, kk = divmod(k, S5_ROWS)
        lhs = jnp.concatenate([x[:, k * bw:(k + 1) * bw], xp[:, k * bw:(k + 1) * bw]], axis=1)
        bu = jnp.dot(lhs, wb_ref[0, k], preferred_element_type=F32)
        for l in range(nl):
            sbufs[e][hb * nl + l, pl.ds(kk, chunk, stride=S5_ROWS), :] = bu[:, l * LANES:(l + 1) * LANES]

    p_r, p_i = [], []
    for hb in range(nhb):
        for l in range(hl):
            a_r = ar_ref[0, hb * S5_ROWS:(hb + 1) * S5_ROWS, l * LANES:(l + 1) * LANES]
            a_i = ai_ref[0, hb * S5_ROWS:(hb + 1) * S5_ROWS, l * LANES:(l + 1) * LANES]
            a_r = jnp.concatenate([a_r, a_r], axis=0)
            a_i = jnp.concatenate([a_i, a_i], axis=0)
            p_r.append(jnp.where(second, a_r * a_r - a_i * a_i, a_r))
            p_i.append(jnp.where(second, 2.0 * a_r * a_i, a_i))
    nch = nhb * hl

    def scan_steps(e, carry, s0, s1):
        for s in range(s0, s1):
            m = s + d * (npair - 1 - 2 * s)
            row = pl.multiple_of(m * 8, 8)
            out = []
            for hb in range(nhb):
                for l in range(hl):
                    ch = hb * hl + l
                    c_r, c_i = carry[2 * ch], carry[2 * ch + 1]
                    pr, pi = hb * nl + l, hb * nl + hl + l
                    n_r = p_r[ch] * c_r - p_i[ch] * c_i + sbufs[e][pr, pl.ds(row, 8), :]
                    n_i = p_r[ch] * c_i + p_i[ch] * c_r + sbufs[e][pi, pl.ds(row, 8), :]
                    sbufs[e][pr, pl.ds(row, 8), :] = n_r
                    sbufs[e][pi, pl.ds(row, 8), :] = n_i
                    out += [jnp.where(second, n_r, pltpu.roll(n_r, S5_ROWS, 0)),
                            jnp.where(second, n_i, pltpu.roll(n_i, S5_ROWS, 0))]
            carry = out
        return carry

    def load_carry(e):
        return [h_ref[e, i] for i in range(2 * nch)]

    def store_carry(e, carry):
        for i in range(2 * nch):
            h_ref[e, i] = carry[i]

    def readout_block(e, k):
        hb, kk = divmod(k, S5_ROWS)
        hk = jnp.concatenate([sbufs[e][hb * nl + l, pl.ds(kk, chunk, stride=S5_ROWS), :]
                              for l in range(nl)], axis=1)
        y_ref[0, e, :, k * bw:(k + 1) * bw] = jnp.dot(hk.astype(BF16), wc_ref[0, k],
                                                      preferred_element_type=F32)

    def run(src_ref, with_readout):
        assert nbat == 2 and npair % nb == 0
        per = npair // nb
        xs = [src_ref[e] for e in range(nbat)]
        xps = [partner_rows(x) for x in xs]
        for k in range(nb):
            project_block(0, xs[0], xps[0], k)
        carry = load_carry(0)
        for k in range(nb):
            project_block(1, xs[1], xps[1], k)
            carry = scan_steps(0, carry, k * per, (k + 1) * per)
        store_carry(0, carry)
        carry = load_carry(1)
        for k in range(nb):
            if with_readout:
                readout_block(0, k)
            carry = scan_steps(1, carry, k * per, (k + 1) * per)
        store_carry(1, carry)
        if with_readout:
            for k in range(nb):
                readout_block(1, k)

    @pl.when(c < n_ctx_chunks)
    def _():
        run(zc_ref, False)

    @pl.when(c >= n_ctx_chunks)
    def _():
        run(zl_ref, True)


def _s5_call(zc, zl, w_b, w_c, a_r, a_i, chunk):
    bsz, lc, width = zc.shape
    seq = zl.shape[1]
    nd, nb, bw2, cw = w_b.shape
    bw = bw2 // 2
    half = cw // 2
    assert nb % S5_ROWS == 0 and lc % chunk == 0 and seq % chunk == 0 and width == nb * bw and chunk % 8 == 0
    ncc, nlc = lc // chunk, seq // chunk
    assert bsz % 2 == 0
    nbat = 2
    nplanes = (nb // S5_ROWS) * (cw // LANES)

    def ctx_idx(b, d, c):
        cc = jnp.minimum(c, ncc - 1)
        return (b, cc + d * (ncc - 1 - 2 * cc), 0)

    def lat_chunk(d, c):
        lc_ = jnp.maximum(c - ncc, 0)
        return lc_ + d * (nlc - 1 - 2 * lc_)

    kern = functools.partial(_s5_kernel, n_ctx_chunks=ncc, chunk=chunk, nb=nb, half=half, nbat=nbat)
    return pl.pallas_call(
        kern,
        grid=(bsz // nbat, nd, ncc + nlc),
        in_specs=[pl.BlockSpec((nbat, chunk, width), ctx_idx),
                  pl.BlockSpec((nbat, chunk, width), lambda b, d, c: (b, lat_chunk(d, c), 0)),
                  pl.BlockSpec((1, nb, bw2, cw), lambda b, d, c: (d, 0, 0, 0)),
                  pl.BlockSpec((1, nb, cw, bw), lambda b, d, c: (d, 0, 0, 0)),
                  pl.BlockSpec((1, nb, half), lambda b, d, c: (d, 0, 0)),
                  pl.BlockSpec((1, nb, half), lambda b, d, c: (d, 0, 0))],
        out_specs=pl.BlockSpec((1, nbat, chunk, width), lambda b, d, c: (d, b, lat_chunk(d, c), 0)),
        out_shape=jax.ShapeDtypeStruct((nd, bsz, seq, width), F32),
        scratch_shapes=[pltpu.VMEM((nplanes, chunk * S5_ROWS, LANES), F32),
                        pltpu.VMEM((nplanes, chunk * S5_ROWS, LANES), F32),
                        pltpu.VMEM((nbat, nplanes, 8, LANES), F32)],
        compiler_params=_params(("arbitrary", "arbitrary", "arbitrary")),
        name="s5_scan",
    )(zc, zl, w_b, w_c, a_r, a_i)


def _mix_kernel(x_ref, yf_ref, yb_ref, us_ref, up_ref, gt_ref, mod_ref, dsk_ref, wglu_ref, bglu_ref,
                pw_ref, psc_ref, pa_ref, pb_ref, wo_ref, nw2_ref, wq_ref,
                x1_ref, h2_ref, q_ref, *, d_model, tm):
    b = pl.program_id(0)
    gate1 = mod_ref[pl.ds(b, 1), 2 * d_model:3 * d_model]
    shift2 = mod_ref[pl.ds(b, 1), 3 * d_model:4 * d_model]
    scale2 = mod_ref[pl.ds(b, 1), 4 * d_model:5 * d_model]

    u_s = us_ref[0].astype(F32)
    y = yf_ref[0, 0] + yb_ref[0, 0] + dsk_ref[...] * u_s
    ge = _gelu_tanh(y)
    lin = jnp.dot(ge.astype(BF16), wglu_ref[...], preferred_element_type=F32) + bglu_ref[...]
    y_a = (ge * _sigmoid(lin)).astype(BF16)

    u_p = up_ref[0]
    ri = lax.broadcasted_iota(I32, (tm, tm), 0)
    ci = lax.broadcasted_iota(I32, (tm, tm), 1)
    same_row = (ri // GRID_W) == (ci // GRID_W)
    pos_r = ri % GRID_W
    pos_c = ci % GRID_W
    pos1 = lax.broadcasted_iota(I32, (tm, 1), 0) % GRID_W
    gw = pw_ref.shape[1]
    y_b_parts = []
    for gi, w in enumerate(POOL_WINDOWS):
        lo = jnp.maximum(pos_r - w // 2, 0)
        hi = jnp.minimum(pos_r + w // 2 - 1, GRID_W - 1)
        band = jnp.where(same_row & (pos_c >= lo) & (pos_c <= hi), 1.0, 0.0).astype(BF16)
        ug = u_p[:, gi * gw:(gi + 1) * gw]
        win = jnp.dot(band, ug, preferred_element_type=F32)
        cnt = (jnp.minimum(pos1 + w // 2 - 1, GRID_W - 1) - jnp.maximum(pos1 - w // 2, 0) + 1).astype(F32)
        p = win / cnt - ug.astype(F32)
        y_b_parts.append(jnp.dot(p.astype(BF16), pw_ref[gi], preferred_element_type=F32))
    y_b = (jnp.concatenate(y_b_parts, axis=-1) * psc_ref[...]).astype(BF16)

    g_a = _sigmoid(gt_ref[0, :, 0:d_model].astype(F32))
    g_b = _sigmoid(gt_ref[0, :, d_model:2 * d_model].astype(F32))
    m = (g_a * jnp.dot(y_a, pa_ref[...], preferred_element_type=F32)
         + g_b * jnp.dot(y_b, pb_ref[...], preferred_element_type=F32))
    x1 = x_ref[0] + gate1 * jnp.dot(m.astype(BF16), wo_ref[...], preferred_element_type=F32)
    x1_ref[0] = x1

    h2 = _modulated_norm(x1, nw2_ref[...], shift2, scale2).astype(BF16)
    h2_ref[0] = h2
    q_ref[0] = jnp.dot(h2, wq_ref[...], preferred_element_type=F32)


def _mix_call(x, y2, us, up, gates, mod, d_skip, w_glu, b_glu, pool_w, pool_scale,
              proj_a, proj_b, w_out, norm2_w, w_q, tm):
    bsz, seq, d_model = x.shape
    sw = us.shape[2]
    pwid = up.shape[2]
    assert seq % tm == 0 and tm % GRID_W == 0
    full = lambda a: pl.BlockSpec(a.shape, lambda b, i: (0,) * a.ndim)
    row = lambda a: a[None]
    tok = lambda wd: pl.BlockSpec((1, tm, wd), lambda b, i: (b, i, 0))
    args = (x, y2, y2, us, up, gates, mod, row(d_skip), w_glu.astype(BF16), row(b_glu),
            pool_w.astype(BF16), row(pool_scale), proj_a.astype(BF16), proj_b.astype(BF16),
            w_out.astype(BF16), row(norm2_w), w_q.astype(BF16))
    in_specs = [tok(d_model),
                pl.BlockSpec((1, 1, tm, sw), lambda b, i: (0, b, i, 0)),
                pl.BlockSpec((1, 1, tm, sw), lambda b, i: (1, b, i, 0)),
                tok(sw), tok(pwid), tok(2 * d_model)] + [full(a) for a in args[6:]]
    return pl.pallas_call(
        functools.partial(_mix_kernel, d_model=d_model, tm=tm),
        grid=(bsz, seq // tm),
        in_specs=in_specs,
        out_specs=[tok(d_model), tok(d_model), tok(d_model)],
        out_shape=[jax.ShapeDtypeStruct((bsz, seq, d_model), F32),
                   jax.ShapeDtypeStruct((bsz, seq, d_model), BF16),
                   jax.ShapeDtypeStruct((bsz, seq, d_model), F32)],
        compiler_params=_params(("arbitrary", "arbitrary")),
        name="mix",
    )(*args)


def _oddeven_merge_sort_pairs(n):
    pairs = []

    def merge(lo, hi, r):
        step = r * 2
        if step < hi - lo:
            merge(lo, hi, step)
            merge(lo + r, hi, step)
            pairs.extend((i, i + r) for i in range(lo + r, hi - r, step))
        else:
            pairs.append((lo, lo + r))

    def sort(lo, hi):
        if hi > lo:
            mid = lo + (hi - lo) // 2
            sort(lo, mid)
            sort(mid + 1, hi)
            merge(lo, hi, 1)

    sort(0, n - 1)
    return pairs


_SORT16_PAIRS = _oddeven_merge_sort_pairs(PEER_KEYS // 8)


def _route_kernel(q_ref, k1_ref, k2_ref, o_ref,
                  v1s, x1s, v2s, x2s, es, gws, tes, tgs, *, tq, rpb):
    neg = -jnp.inf
    nk = PEER_KEYS
    ns = PEER_HEADS * PEER_TOPK
    sub8 = lax.broadcasted_iota(I32, (8, tq), 0)
    lt2, lt4, lt5, lt6 = sub8 < 2, sub8 < 4, sub8 < 5, sub8 < 6
    s8 = sub8.astype(F32)
    cio = jnp.concatenate([s8, s8 + 8.0, s8 + 16.0,
                           jnp.where(lt5, s8 + 32.0, s8 + 59.0),
                           jnp.where(lt4, s8 + 48.0, jnp.where(lt6, s8 + 76.0, s8 + 90.0)),
                           jnp.where(lt2, s8 + 112.0, s8 * 16.0 + 96.0),
                           s8 * 16.0 + 224.0], axis=0)
    big = 1024.0
    sub_s = lax.broadcasted_iota(I32, (nk, ns), 0).astype(F32)
    hd = q_ref.shape[1] // PEER_HEADS
    half = hd // 2
    tokens_per_head = tq // PEER_HEADS

    @pl.when(pl.program_id(0) == 0)
    def _():
        tes[...] = jnp.zeros_like(tes)
        tgs[...] = jnp.zeros_like(tgs)

    def top16(s, vals_ref, idx_ref):
        val = [s[8 * v:8 * v + 8, :] for v in range(nk // 8)]
        idx = [s8 + float(8 * v) for v in range(nk // 8)]
        for (a, b) in _SORT16_PAIRS:
            va, vb, ia, ib = val[a], val[b], idx[a], idx[b]
            first = (va > vb) | ((va == vb) & (ia < ib))
            val[a], val[b] = jnp.maximum(va, vb), jnp.minimum(va, vb)
            idx[a], idx[b] = jnp.where(first, ia, ib), jnp.where(first, ib, ia)
        for k in range(PEER_TOPK):
            m = jnp.max(val[0], axis=0, keepdims=True)
            i = jnp.min(jnp.where(val[0] == m, idx[0], big), axis=0, keepdims=True)
            vals_ref[k:k + 1, :] = m
            idx_ref[k:k + 1, :] = i
            win = idx[0] == i
            for d in range(PEER_TOPK - 1 - k):
                val[d] = jnp.where(win, val[d + 1], val[d])
                idx[d] = jnp.where(win, idx[d + 1], idx[d])

    def route_head(h):
        q1 = q_ref[:, h * hd:h * hd + half]
        q2 = q_ref[:, h * hd + half:(h + 1) * hd]
        nt = (((1,), (1,)), ((), ()))
        s1 = lax.dot_general(k1_ref[...], q1, nt, precision=lax.Precision.HIGHEST,
                             preferred_element_type=F32)
        s2 = lax.dot_general(k2_ref[...], q2, nt, precision=lax.Precision.HIGHEST,
                             preferred_element_type=F32)
        top16(s1, v1s, x1s)
        top16(s2, v2s, x2s)

        v2a, v2b = v2s[0:8, :], v2s[8:16, :]
        x2a, x2b = x2s[0:8, :], x2s[8:16, :]
        r1v = pltpu.roll(v1s[8:16, :], 2, 0)
        r1x = pltpu.roll(x1s[8:16, :], 2, 0)
        v1r = lambda p: v1s[p:p + 1, :]
        x1r = lambda p: x1s[p:p + 1, :]
        pieces = [
            (v1r(0) + v2a, x1r(0), x2a),
            (v1r(0) + v2b, x1r(0), x2b),
            (v1r(1) + v2a, x1r(1), x2a),
            (jnp.where(lt5, v1r(2), v1r(4)) + jnp.where(lt5, v2a, pltpu.roll(v2a, 5, 0)),
             jnp.where(lt5, x1r(2), x1r(4)), jnp.where(lt5, x2a, pltpu.roll(x2a, 5, 0))),
            (jnp.where(lt4, v1r(3), jnp.where(lt6, v1r(5), v1r(6)))
             + jnp.where(lt4, v2a, jnp.where(lt6, pltpu.roll(v2a, 4, 0), pltpu.roll(v2a, 6, 0))),
             jnp.where(lt4, x1r(3), jnp.where(lt6, x1r(5), x1r(6))),
             jnp.where(lt4, x2a, jnp.where(lt6, pltpu.roll(x2a, 4, 0), pltpu.roll(x2a, 6, 0)))),
            (jnp.where(lt2, v1r(7), r1v) + jnp.where(lt2, v2a, v2s[0:1, :]),
             jnp.where(lt2, x1r(7), r1x), jnp.where(lt2, x2a, x2s[0:1, :])),
            (jnp.where(lt2, r1v + v2s[0:1, :], neg), r1x, jnp.broadcast_to(x2s[0:1, :], (8, tq))),
        ]
        cand = jnp.concatenate([pc[0] for pc in pieces], axis=0)
        ceid = jnp.concatenate([jnp.broadcast_to(pc[1], (8, tq)) * float(nk) + pc[2] for pc in pieces], axis=0)

        tops = []
        for k in range(PEER_TOPK):
            m = jnp.max(cand, axis=0, keepdims=True)
            i = jnp.min(jnp.where(cand == m, cio, big), axis=0, keepdims=True)
            hit = cio == i
            r = h * PEER_TOPK + k
            es[r:r + 1, :] = jnp.sum(jnp.where(hit, ceid, 0.0), axis=0, keepdims=True)
            tops.append(m)
            cand = jnp.where(hit, neg, cand)
        ex = [jnp.exp(t - tops[0]) for t in tops]
        tot = ex[0]
        for e in ex[1:]:
            tot = tot + e
        for k in range(PEER_TOPK):
            r = h * PEER_TOPK + k
            gws[r:r + 1, :] = ex[k] / tot

    def expand_tokens(t0):
        eid_8 = tes[t0:t0 + 8, :]
        i1_8 = jnp.floor(eid_8 * (1.0 / nk))
        i2_8 = eid_8 - i1_8 * float(nk)
        gw_8 = tgs[t0:t0 + 8, :]
        gh_8 = gw_8.astype(BF16).astype(F32)
        gl_8 = gw_8 - gh_8
        for u in range(8):
            hit1 = sub_s == i1_8[u:u + 1, :]
            a2 = jnp.concatenate([jnp.where(hit1, gh_8[u:u + 1, :], 0.0),
                                  jnp.where(hit1, gl_8[u:u + 1, :], 0.0)], axis=1).astype(BF16)
            b1 = jnp.where(sub_s == i2_8[u:u + 1, :], 1.0, 0.0)
            b2 = jnp.concatenate([b1, b1], axis=1).astype(BF16)
            gt = lax.dot_general(a2, b2, (((1,), (1,)), ((), ())), preferred_element_type=F32)
            row0 = (t0 + u) * rpb
            for jb in range(nk // rpb):
                o_ref[0, jb, row0:row0 + rpb, :] = gt[jb * rpb:(jb + 1) * rpb, :]

    for h in range(PEER_HEADS):
        route_head(h)
        for t0 in range(h * tokens_per_head, (h + 1) * tokens_per_head, 8):
            expand_tokens(t0)

    tes[...] = es[...].T
    tgs[...] = gws[...].T


def _route_call(q2d, k1, k2, tq, tm, rpb):
    n_tok, d_q = q2d.shape
    nk = PEER_KEYS
    ns = PEER_HEADS * PEER_TOPK
    assert n_tok % tm == 0 and tm % tq == 0 and ns == nk and tq == LANES and nk % rpb == 0
    n_tiles = n_tok // tq
    per = tm // tq

    def out_idx(i):
        p = jnp.maximum(i - 1, 0)
        return (p // per, 0, p % per, 0)

    return pl.pallas_call(
        functools.partial(_route_kernel, tq=tq, rpb=rpb),
        grid=(n_tiles + 1,),
        in_specs=[pl.BlockSpec((tq, d_q), lambda i: (jnp.minimum(i, n_tiles - 1), 0)),
                  pl.BlockSpec(k1.shape, lambda i: (0, 0)),
                  pl.BlockSpec(k2.shape, lambda i: (0, 0))],
        out_specs=pl.BlockSpec((1, nk // rpb, tq * rpb, nk), out_idx),
        out_shape=jax.ShapeDtypeStruct((n_tok // tm, nk // rpb, tm * rpb, nk), F32),
        scratch_shapes=[pltpu.VMEM((PEER_TOPK, tq), F32), pltpu.VMEM((PEER_TOPK, tq), F32),
                        pltpu.VMEM((PEER_TOPK, tq), F32), pltpu.VMEM((PEER_TOPK, tq), F32),
                        pltpu.VMEM((ns, tq), F32), pltpu.VMEM((ns, tq), F32),
                        pltpu.VMEM((tq, ns), F32), pltpu.VMEM((tq, ns), F32)],
        compiler_params=_params(("arbitrary",)),
        name="peer_route",
    )(q2d, k1, k2)


def _peer_kernel(h_ref, u_ref, v_ref, g_ref, x1_ref, mod_ref, fw_ref, o_ref,
                 *, d_model, rpb, tokens_per_batch, tm):
    i = pl.program_id(0)
    j = pl.program_id(1)

    @pl.when(j == 0)
    def _():
        o_ref[...] = jnp.zeros_like(o_ref)

    z = lax.dot_general(h_ref[...], u_ref[...], (((1,), (1,)), ((), ())), preferred_element_type=F32)
    gmat = jnp.concatenate([g_ref[0, 0, pl.ds(r, tm, stride=rpb), :] for r in range(rpb)], axis=1)
    w = (_gelu_tanh(z) * gmat).astype(BF16)
    o_ref[...] += jnp.dot(w, v_ref[...], preferred_element_type=F32)

    @pl.when(j == pl.num_programs(1) - 1)
    def _():
        b = (i * tm) // tokens_per_batch
        gate2 = mod_ref[pl.ds(b, 1), 5 * d_model:6 * d_model]
        x2 = x1_ref[...] + gate2 * o_ref[...]
        ms = jnp.mean(x2 * x2, axis=-1, keepdims=True)
        o_ref[...] = x2 * lax.rsqrt(ms + EPS) * fw_ref[...]


def _peer_call(h2, u_bf, v_bf, gmat, x1, mod, final_w, tokens_per_batch, tm, rpb):
    n_tok, d_model = h2.shape
    n_exp = v_bf.shape[0]
    nk = PEER_KEYS
    eb = rpb * nk
    assert n_tok % tm == 0 and n_exp % eb == 0 and tokens_per_batch % tm == 0
    assert gmat.shape == (n_tok // tm, n_exp // eb, tm * rpb, nk)
    kern = functools.partial(_peer_kernel, d_model=d_model, rpb=rpb,
                             tokens_per_batch=tokens_per_batch, tm=tm)
    once = pl.Buffered(1)
    return pl.pallas_call(
        kern,
        grid=(n_tok // tm, n_exp // eb),
        in_specs=[pl.BlockSpec((tm, d_model), lambda i, j: (i, 0), pipeline_mode=once),
                  pl.BlockSpec((eb, d_model), lambda i, j: (j, 0)),
                  pl.BlockSpec((eb, d_model), lambda i, j: (j, 0)),
                  pl.BlockSpec((1, 1, tm * rpb, nk), lambda i, j: (i, j, 0, 0)),
                  pl.BlockSpec((tm, d_model), lambda i, j: (i, 0), pipeline_mode=once),
                  pl.BlockSpec(mod.shape, lambda i, j: (0, 0)),
                  pl.BlockSpec((1, d_model), lambda i, j: (0, 0))],
        out_specs=pl.BlockSpec((tm, d_model), lambda i, j: (i, 0), pipeline_mode=once),
        out_shape=jax.ShapeDtypeStruct((n_tok, d_model), F32),
        compiler_params=_params(("arbitrary", "arbitrary"), WIDE_VMEM_LIMIT),
        name="peer_experts",
    )(h2, u_bf, v_bf, gmat, x1, mod, final_w[None])


def kernel(x, c, ctx, c_ctx, w_mod, b_mod, norm1_w, norm2_w, w_in, s5_a_re, s5_a_im, s5_log_dt,
           s5_b_re, s5_b_im, s5_c_re, s5_c_im, s5_d, w_glu, b_glu, pool_w, pool_scale, proj_a,
           proj_b, w_out, peer_wq, peer_k1, peer_k2, peer_u, peer_v, final_w):
    bsz, seq, d_model = x.shape
    depth = w_mod.shape[0]
    assert depth == 1, "only the single-layer configuration is implemented"
    l = 0
    s5_width = s5_d.shape[1]
    pool_width = pool_scale.shape[1]

    mod = _mod_call(c, c_ctx, w_mod[l], b_mod[l])

    tiles = _tile_plan(seq, ctx.shape[1])
    w_in_bf = w_in[l].astype(BF16)
    us, up, gates = _inproj_call(x, mod, norm1_w[l], w_in_bf,
                                 (s5_width, pool_width, 2 * d_model), None, tiles.proj, "inproj_lat")
    (zc,) = _inproj_call(ctx, mod, norm1_w[l], w_in_bf[:, :s5_width], (s5_width,), bsz,
                         tiles.ctx, "inproj_ctx")

    disc = _disc_call(s5_a_re[l], s5_a_im[l], s5_log_dt[l], s5_b_re[l], s5_b_im[l])
    w_b, w_c, a_r, a_i = _s5_weights(*disc, s5_c_re[l], s5_c_im[l])
    y2 = _s5_call(zc, us, w_b, w_c, a_r, a_i, tiles.chunk)

    x1, h2, q = _mix_call(x, y2, us, up, gates, mod, s5_d[l], w_glu[l], b_glu[l], pool_w[l],
                          pool_scale[l], proj_a[l], proj_b[l], w_out[l], norm2_w[l], peer_wq[l], tiles.tok)

    n_tok = bsz * seq
    gmat = _route_call(q.reshape(n_tok, d_model), peer_k1[l], peer_k2[l], tiles.route, tiles.peer, tiles.rpb)
    out = _peer_call(h2.reshape(n_tok, d_model), peer_u[l].astype(BF16), peer_v[l].astype(BF16),
                     gmat, x1.reshape(n_tok, d_model), mod, final_w, seq, tiles.peer, tiles.rpb)
    return out.reshape(bsz, seq, d_model)
```
